```python
import math
import jax, jax.numpy as jnp
from jax import lax
import numpy as np

D_MODEL = 1024
BATCH = 16
SEQ = 2048
DEPTH = 1
DEC_BATCH = 32
DEC_SEQ = 1
PAST_LEN = 16384
PAGE_SIZE = 128

HG_HEADS = 4
HG_DK = 128
HG_DV = 128
HG_CHUNK = 64
DA_HEADS = 4
DA_DH = 64
DA_DK = 2 * DA_DH
DA_DV = 128
ROPE_THETA = 10000.0
Q_BLOCK = 128
MIX_W = HG_HEADS * HG_DV + DA_HEADS * DA_DV
IN_SIZES = (HG_HEADS * HG_DK, HG_HEADS * HG_DK, HG_HEADS * HG_DV, HG_HEADS * HG_DV,
            DA_HEADS * DA_DK, DA_HEADS * DA_DK, DA_HEADS * DA_DV)
IN_COLS = sum(IN_SIZES)
PEER_HEADS = 8
PEER_NKEYS = 128
PEER_EXPERTS = PEER_NKEYS * PEER_NKEYS
PEER_QHALF = 128
PEER_TOPK = 16
PEER_BLOCK = 256
PLE_DIM = 256
EPS = 1e-6

kernel_name = "hymba_hgrn2_diffattn_peer_step"

F32 = jnp.float32


def rmsnorm(x, g):
    xf = x.astype(F32)
    y = xf * lax.rsqrt(jnp.mean(xf * xf, axis=-1, keepdims=True) + EPS)
    return (y * g.astype(F32)).astype(x.dtype)


def rope(x, pos):
    half = DA_DH // 2
    freqs = ROPE_THETA ** (-jnp.arange(half, dtype=F32) / half)
    ang = pos.astype(F32)[:, None] * freqs[None, :]
    cos = jnp.cos(ang)[None, :, None, None, :]
    sin = jnp.sin(ang)[None, :, None, None, :]
    xf = x.astype(F32)
    x1, x2 = xf[..., :half], xf[..., half:]
    return jnp.concatenate([x1 * cos - x2 * sin, x2 * cos + x1 * sin], axis=-1).astype(x.dtype)


def mixer_inputs(xn, w_in, hgrn_gamma, layer, pos):
    B, L, _ = xn.shape
    proj = xn @ w_in
    idx = np.cumsum(IN_SIZES)[:-1].tolist()
    hq, hf, hi, hg, aq, ak, av = jnp.split(proj, idx, axis=-1)
    lb = jnp.cumsum(jax.nn.softmax(hgrn_gamma.astype(F32), axis=0), axis=0)[layer]
    f = lb + (1.0 - lb) * jax.nn.sigmoid(hf.astype(F32))
    logf = jnp.log(f)
    hk = 1.0 - f
    hq = hq.reshape(B, L, HG_HEADS, HG_DK)
    hk = hk.reshape(B, L, HG_HEADS, HG_DK)
    logf = logf.reshape(B, L, HG_HEADS, HG_DK)
    hi = hi.reshape(B, L, HG_HEADS, HG_DV)
    aq = rope(aq.reshape(B, L, DA_HEADS, 2, DA_DH), pos)
    ak = rope(ak.reshape(B, L, DA_HEADS, 2, DA_DH), pos)
    av = av.reshape(B, L, DA_HEADS, DA_DV)
    return hq, hk, hi, logf, hg, aq, ak, av


def hgrn2_recurrence(q, k, v, logf, s0):
    B, L, H, K = q.shape
    V = v.shape[-1]
    C = math.gcd(L, HG_CHUNK)
    NC = L // C

    def chunks(t):
        return t.astype(F32).reshape(B, NC, C, H, t.shape[-1]).transpose(1, 0, 3, 2, 4)

    causal = jnp.tril(jnp.ones((C, C), dtype=bool))

    def step(S, inp):
        qc, kc, vc, gc = inp
        G = jnp.cumsum(gc, axis=2)
        o_inter = jnp.einsum('bhtk,bhkv->bhtv', qc * jnp.exp(G), S)
        diff = G[:, :, :, None, :] - G[:, :, None, :, :]
        decay = jnp.exp(jnp.where(causal[:, :, None], diff, -jnp.inf))
        A = jnp.einsum('bhtk,bhsk,bhtsk->bhts', qc, kc, decay)
        o = o_inter + jnp.einsum('bhts,bhsv->bhtv', A, vc)
        G_last = G[:, :, -1:, :]
        S_new = jnp.exp(G_last[:, :, 0, :])[..., None] * S + jnp.einsum(
            'bhsk,bhsv->bhkv', kc * jnp.exp(G_last - G), vc)
        return S_new, o

    S, o = lax.scan(step, s0.astype(F32), (chunks(q), chunks(k), chunks(v), chunks(logf)))
    o = o.transpose(1, 0, 3, 2, 4).reshape(B, L, H, V)
    return o, S


def diff_lambda(lq1, lk1, lq2, lk2, lam_init):
    return (jnp.exp(jnp.sum(lq1.astype(F32) * lk1.astype(F32)))
            - jnp.exp(jnp.sum(lq2.astype(F32) * lk2.astype(F32))) + lam_init)


def diff_attn_prompt(q, k, v, lam):
    B, L = q.shape[:2]
    QB = math.gcd(L, Q_BLOCK)
    NB = L // QB
    scale = DA_DH ** -0.5
    kpos = jnp.arange(L)
    qblocks = q.reshape(B, NB, QB, DA_HEADS, 2, DA_DH).swapaxes(0, 1)

    def block(args):
        qb, start = args
        s = jnp.einsum('bqhcd,bkhcd->bchqk', qb, k, preferred_element_type=F32) * scale
        mask = kpos[None, :] <= (start + jnp.arange(QB))[:, None]
        p = jax.nn.softmax(jnp.where(mask, s, -jnp.inf), axis=-1)
        a = p[:, 0] - lam * p[:, 1]
        return jnp.einsum('bhqk,bkhd->bqhd', a, v.astype(F32)).astype(v.dtype)

    o = lax.map(block, (qblocks, jnp.arange(NB) * QB))
    return o.swapaxes(0, 1).reshape(B, L, DA_HEADS, DA_DV)


def diff_attn_sample(q, k, v, k_past, v_past, lam):
    Lq = q.shape[1]
    P = k_past.shape[1]
    scale = DA_DH ** -0.5
    s_past = jnp.einsum('bqhcd,bkhcd->bchqk', q, k_past, preferred_element_type=F32) * scale
    s_new = jnp.einsum('bqhcd,bkhcd->bchqk', q, k, preferred_element_type=F32) * scale
    causal = jnp.tril(jnp.ones((Lq, Lq), dtype=bool))
    s_new = jnp.where(causal, s_new, -jnp.inf)
    p = jax.nn.softmax(jnp.concatenate([s_past, s_new], axis=-1), axis=-1)
    a = p[:, 0] - lam * p[:, 1]
    o = (jnp.einsum('bhqk,bkhd->bqhd', a[..., :P], v_past.astype(F32))
         + jnp.einsum('bhqk,bkhd->bqhd', a[..., P:], v.astype(F32)))
    return o.astype(v.dtype)


def mix_out(o_h, hg, o_d, g_h, g_d, lam_init, w_out):
    B, L = o_h.shape[:2]
    gate = jax.nn.silu(hg.astype(F32)).reshape(B, L, HG_HEADS, HG_DV)
    yh = rmsnorm(o_h, g_h).astype(F32) * gate
    yd = rmsnorm(o_d, g_d).astype(F32) * (1.0 - lam_init)
    y = jnp.concatenate([yh.reshape(B, L, -1), yd.reshape(B, L, -1)], axis=-1)
    return y.astype(w_out.dtype) @ w_out


def peer(x, w_query, sub_keys, u_table, v_table):
    T, D = x.shape
    TB = math.gcd(T, PEER_BLOCK)
    xb = x.reshape(T // TB, TB, D)

    def blk(xt):
        qh = (xt @ w_query).reshape(TB, PEER_HEADS, 2, PEER_QHALF)
        s = jnp.einsum('thck,hcnk->thcn', qh, sub_keys, preferred_element_type=F32)
        s1, i1 = lax.top_k(s[:, :, 0], PEER_TOPK)
        s2, i2 = lax.top_k(s[:, :, 1], PEER_TOPK)
        cand = (s1[..., :, None] + s2[..., None, :]).reshape(TB, PEER_HEADS, PEER_TOPK * PEER_TOPK)
        cidx = (i1[..., :, None] * PEER_NKEYS + i2[..., None, :]).reshape(TB, PEER_HEADS, PEER_TOPK * PEER_TOPK)
        sc, pos = lax.top_k(cand, PEER_TOPK)
        eidx = jnp.take_along_axis(cidx, pos, axis=-1)
        g = jax.nn.softmax(sc, axis=-1)
        u = u_table[eidx]
        a = jax.nn.gelu(jnp.einsum('td,thed->the', xt, u, preferred_element_type=F32))
        vv = v_table[eidx]
        return jnp.einsum('the,thed->td', g * a, vv.astype(F32)).astype(x.dtype)

    return lax.map(blk, xb).reshape(T, D)


def ffn_ple(h, p, g_ffn, wq, sk, u, v, wg, bg, wp):
    B, L, D = h.shape
    hn = rmsnorm(h, g_ffn)
    h = h + peer(hn.reshape(B * L, D), wq, sk, u, v).reshape(B, L, D)
    gate = jax.nn.sigmoid((h @ wg + bg).astype(F32))
    return h + (gate * (p @ wp).astype(F32)).astype(h.dtype)


def setup_inputs(seed: int = 0) -> dict:
    key = jax.random.key(seed)
    ks = jax.random.split(key, 32)
    n_pages = PAST_LEN // PAGE_SIZE
    n_used = DEC_BATCH * n_pages
    n_pool = n_used + n_used // 4
    nrm = lambda k, shp, s: jax.random.normal(k, shp, F32) * s
    page_table = jax.random.permutation(ks[0], n_pool)[:n_used].reshape(DEC_BATCH, n_pages).astype(jnp.int32)
    return {
        "x_prompt": nrm(ks[1], (BATCH, SEQ, D_MODEL), 1.0),
        "x_sample": nrm(ks[2], (DEC_BATCH, DEC_SEQ, D_MODEL), 1.0),
        "p_prompt": nrm(ks[3], (DEPTH, BATCH, SEQ, PLE_DIM), 1.0),
        "p_sample": nrm(ks[4], (DEPTH, DEC_BATCH, DEC_SEQ, PLE_DIM), 1.0),
        "cache_k": nrm(ks[5], (DEPTH, n_pool, PAGE_SIZE, DA_HEADS, DA_DK), 1.0),
        "cache_v": nrm(ks[6], (DEPTH, n_pool, PAGE_SIZE, DA_HEADS, DA_DV), 1.0),
        "state_hgrn": nrm(ks[7], (DEPTH, DEC_BATCH, HG_HEADS, HG_DK, HG_DV), 0.5),
        "page_table": page_table,
        "g_attn": 1.0 + nrm(ks[8], (DEPTH, D_MODEL), 0.01),
        "w_in": nrm(ks[9], (DEPTH, D_MODEL, IN_COLS), D_MODEL ** -0.5),
        "hgrn_gamma": nrm(ks[10], (DEPTH + 1, HG_HEADS * HG_DK), 0.5),
        "g_hgrn_norm": 1.0 + nrm(ks[11], (DEPTH, HG_DV), 0.01),
        "lambda_q1": nrm(ks[12], (DEPTH, DA_DH), 0.1),
        "lambda_k1": nrm(ks[13], (DEPTH, DA_DH), 0.1),
        "lambda_q2": nrm(ks[14], (DEPTH, DA_DH), 0.1),
        "lambda_k2": nrm(ks[15], (DEPTH, DA_DH), 0.1),
        "g_diff_norm": 1.0 + nrm(ks[16], (DEPTH, DA_DV), 0.01),
        "w_out": nrm(ks[17], (DEPTH, MIX_W, D_MODEL), MIX_W ** -0.5),
        "g_ffn": 1.0 + nrm(ks[18], (DEPTH, D_MODEL), 0.01),
        "peer_w_query": nrm(ks[19], (DEPTH, D_MODEL, PEER_HEADS * 2 * PEER_QHALF), D_MODEL ** -0.5),
        "peer_sub_keys": nrm(ks[20], (DEPTH, PEER_HEADS, 2, PEER_NKEYS, PEER_QHALF), PEER_QHALF ** -0.5),
        "peer_u": nrm(ks[21], (DEPTH, PEER_EXPERTS, D_MODEL), D_MODEL ** -0.5),
        "peer_v": nrm(ks[22], (DEPTH, PEER_EXPERTS, D_MODEL), 0.5),
        "ple_w_gate": nrm(ks[23], (DEPTH, D_MODEL, D_MODEL), D_MODEL ** -0.5),
        "ple_b_gate": nrm(ks[24], (DEPTH, D_MODEL), 0.01),
        "ple_w_proj": nrm(ks[25], (DEPTH, PLE_DIM, D_MODEL), PLE_DIM ** -0.5),
        "g_final": 1.0 + nrm(ks[26], (D_MODEL,), 0.01),
    }


def reference(x_prompt, x_sample, p_prompt, p_sample, cache_k, cache_v, state_hgrn, page_table,
              g_attn, w_in, hgrn_gamma, g_hgrn_norm, lambda_q1, lambda_k1, lambda_q2, lambda_k2,
              g_diff_norm, w_out, g_ffn, peer_w_query, peer_sub_keys, peer_u, peer_v,
              ple_w_gate, ple_b_gate, ple_w_proj, g_final):
    Bp, Lp, _ = x_prompt.shape
    Bs, Ls, _ = x_sample.shape
    past_len = page_table.shape[1] * cache_k.shape[2]
    pos_p = jnp.arange(Lp)
    pos_s = past_len + jnp.arange(Ls)
    hp, hs = x_prompt, x_sample
    kp_rows, vp_rows, sp_list, ks_rows, vs_rows, ss_list = [], [], [], [], [], []
    for l in range(DEPTH):
        lam_init = 0.8 - 0.6 * math.exp(-0.3 * l)
        lam = diff_lambda(lambda_q1[l], lambda_k1[l], lambda_q2[l], lambda_k2[l], lam_init)
        xn = rmsnorm(hp, g_attn[l])
        hq, hk, hi, logf, hg, aq, ak, av = mixer_inputs(xn, w_in[l], hgrn_gamma, l, pos_p)
        s0 = jnp.zeros((Bp, HG_HEADS, HG_DK, HG_DV), F32)
        o_h, S_p = hgrn2_recurrence(hq, hk, hi, logf, s0)
        o_d = diff_attn_prompt(aq, ak, av, lam)
        hp = hp + mix_out(o_h, hg, o_d, g_hgrn_norm[l], g_diff_norm[l], lam_init, w_out[l])
        hp = ffn_ple(hp, p_prompt[l], g_ffn[l], peer_w_query[l], peer_sub_keys[l], peer_u[l], peer_v[l],
                     ple_w_gate[l], ple_b_gate[l], ple_w_proj[l])
        kp_rows.append(ak.reshape(Bp, Lp, DA_HEADS, DA_DK))
        vp_rows.append(av)
        sp_list.append(S_p)
        xn = rmsnorm(hs, g_attn[l])
        hq, hk, hi, logf, hg, aq, ak, av = mixer_inputs(xn, w_in[l], hgrn_gamma, l, pos_s)
        o_h, S_s = hgrn2_recurrence(hq, hk, hi, logf, state_hgrn[l])
        k_past = cache_k[l, page_table].reshape(Bs, past_len, DA_HEADS, 2, DA_DH)
        v_past = cache_v[l, page_table].reshape(Bs, past_len, DA_HEADS, DA_DV)
        o_d = diff_attn_sample(aq, ak, av, k_past, v_past, lam)
        hs = hs + mix_out(o_h, hg, o_d, g_hgrn_norm[l], g_diff_norm[l], lam_init, w_out[l])
        hs = ffn_ple(hs, p_sample[l], g_ffn[l], peer_w_query[l], peer_sub_keys[l], peer_u[l], peer_v[l],
                     ple_w_gate[l], ple_b_gate[l], ple_w_proj[l])
        ks_rows.append(ak.reshape(Bs, Ls, DA_HEADS, DA_DK))
        vs_rows.append(av)
        ss_list.append(S_s)
    y_prompt = rmsnorm(hp, g_final)
    y_sample = rmsnorm(hs, g_final)
    k_prompt = jnp.stack(kp_rows, axis=0)
    v_prompt = jnp.stack(vp_rows, axis=0)
    state_prompt = jnp.stack(sp_list, axis=0)
    k_sample = jnp.stack(ks_rows, axis=0)
    v_sample = jnp.stack(vs_rows, axis=0)
    state_sample = jnp.stack(ss_list, axis=0)
    return (y_prompt, y_sample, k_prompt, v_prompt, state_prompt, k_sample, v_sample, state_sample)
```

```python
import functools
import math

import numpy as np
import jax
import jax.numpy as jnp
from jax import lax
from jax.experimental import pallas as pl
from jax.experimental.pallas import tpu as pltpu

F32 = jnp.float32
BF16 = jnp.bfloat16
I32 = jnp.int32

D_MODEL = 1024
HEADS = 4
HEAD_W = 128
GROUP_W = HEADS * HEAD_W
DA_DH = 64
ROPE_THETA = 10000.0
HG_CHUNK = 64
PEER_HEADS = 8
PEER_NKEYS = 128
PEER_TOPK = 16
PEER_SEL = PEER_HEADS * PEER_TOPK
PLE_DIM = 256
EPS = 1e-6
PAGE = 128
PAGES_PER_STEP = 8
ROW_CHUNKS = D_MODEL // HEAD_W
WORD_ROWS = ROW_CHUNKS // 2
VMEM_LIMIT = 56 * 1024 * 1024

NT = (((1,), (1,)), ((), ()))
TN = (((0,), (0,)), ((), ()))


def _cparams(sem):
    return pltpu.CompilerParams(dimension_semantics=sem, vmem_limit_bytes=VMEM_LIMIT)


def _rms(x, g):
    return x * lax.rsqrt(jnp.mean(x * x, axis=-1, keepdims=True) + EPS) * g


def _proj_kernel(x_ref, g_ref, w_ref, lb_ref, cos_ref, sin_ref,
                 hq_ref, hk_ref, lf_ref, hi_ref, gate_ref, k_ref, v_ref,
                 qb_ref, kb_ref, vb_ref):
    xb = _rms(x_ref[...], g_ref[...]).astype(BF16)

    def col(i):
        return jnp.dot(xb, w_ref[:, i * GROUP_W:(i + 1) * GROUP_W], preferred_element_type=F32)

    hq_ref[...] = col(0)
    lb = lb_ref[...]
    f = lb + (1.0 - lb) * jax.nn.sigmoid(col(1))
    lf_ref[...] = jnp.log(f)
    hk_ref[...] = 1.0 - f
    hi_ref[...] = col(2)
    gate_ref[...] = jax.nn.silu(col(3))

    cosf = cos_ref[...]
    sins = sin_ref[...]
    lane = lax.broadcasted_iota(I32, cosf.shape, 1)
    first_half = (lane % DA_DH) < (DA_DH // 2)

    def rope(x):
        swapped = jnp.where(first_half,
                            pltpu.roll(x, GROUP_W - DA_DH // 2, 1),
                            pltpu.roll(x, DA_DH // 2, 1))
        return x * cosf + swapped * sins

    q = rope(col(4))
    qb_ref[...] = (q * (DA_DH ** -0.5)).astype(BF16)
    k = rope(col(5))
    k_ref[...] = k
    kb_ref[...] = k.astype(BF16)
    v = col(6)
    v_ref[...] = v
    vb_ref[...] = v.astype(BF16)


def _proj(x, g, w_bf, lb, cosf, sins, tm):
    T = x.shape[0]
    nl = cosf.shape[0] // tm
    row = lambda i: (i, 0)
    fixed = lambda i: (0, 0)
    tab = lambda i: (i % nl, 0)
    f32o = jax.ShapeDtypeStruct((T, GROUP_W), F32)
    bfo = jax.ShapeDtypeStruct((T, GROUP_W), BF16)
    ospec = pl.BlockSpec((tm, GROUP_W), row)
    return pl.pallas_call(
        _proj_kernel,
        grid=(T // tm,),
        in_specs=[pl.BlockSpec((tm, D_MODEL), row),
                  pl.BlockSpec((1, D_MODEL), fixed),
                  pl.BlockSpec(w_bf.shape, fixed),
                  pl.BlockSpec((1, GROUP_W), fixed),
                  pl.BlockSpec((tm, GROUP_W), tab),
                  pl.BlockSpec((tm, GROUP_W), tab)],
        out_specs=[ospec] * 10,
        out_shape=[f32o] * 7 + [bfo] * 3,
        compiler_params=_cparams(("parallel",)),
        name="proj",
    )(x, g, w_bf, lb, cosf, sins)


def _hgrn_consts():
    C = HG_CHUNK
    t = np.arange(C)[:, None]
    u = np.arange(C)[None, :]
    mats = [(u <= t)]
    lows, pms = [], []
    h = C // 2
    while h >= 1:
        base = (t // (2 * h)) * (2 * h)
        r = base + h - 1
        lower = t >= base + h
        m = np.where(lower, (u > r) & (u <= t), (u > t) & (u <= r))
        mats.append(m)
        lows.append(np.broadcast_to(lower, (C, HEAD_W)))
        s = np.arange(C)[None, :]
        pms.append((t // (2 * h)) == (s // (2 * h)))
        h //= 2
    mats.append(u > t)
    wall = np.concatenate(mats, axis=0).astype(np.float32)
    return (jnp.asarray(wall, BF16), jnp.asarray(np.stack(lows), F32), jnp.asarray(np.stack(pms), F32))


def _split3(x):
    a = x.astype(BF16)
    r = x - a.astype(F32)
    b = r.astype(BF16)
    c = (r - b.astype(F32)).astype(BF16)
    return a, b, c


def _hgrn_kernel(q_ref, k_ref, v_ref, lf_ref, wall_ref, low_ref, pm_ref, o_ref, st_ref, st_scr, *, nlev):
    i = pl.program_id(2)
    C = HG_CHUNK

    @pl.when(i == 0)
    def _():
        st_scr[...] = jnp.zeros_like(st_scr)

    wall = wall_ref[...]
    eye = (lax.broadcasted_iota(I32, (C, C), 0) == lax.broadcasted_iota(I32, (C, C), 1)).astype(F32)
    st = st_scr[...]
    for c in range(q_ref.shape[0] // C):
        sl = slice(c * C, (c + 1) * C)
        q, k, v, lf = q_ref[sl, :], k_ref[sl, :], v_ref[sl, :], lf_ref[sl, :]
        lcat = jnp.concatenate(_split3(lf), axis=-1)
        d3 = jnp.dot(wall, lcat, preferred_element_type=F32)
        e_all = jnp.exp(d3[:, :HEAD_W] + d3[:, HEAD_W:2 * HEAD_W] + d3[:, 2 * HEAD_W:])
        e_g = e_all[0:C]
        e_last = e_all[C - 1:C]
        e_k = e_all[(nlev + 1) * C:(nlev + 2) * C]
        vb = v.astype(BF16)
        o = lax.dot_general((q * e_g).astype(BF16), st.astype(BF16), NT, preferred_element_type=F32)
        a = eye * jnp.sum(q * k, axis=-1, keepdims=True)
        for l in range(nlev):
            e_l = e_all[(l + 1) * C:(l + 2) * C]
            low = low_ref[l]
            ql = (q * e_l * low).astype(BF16)
            kl = (k * e_l * (1.0 - low)).astype(BF16)
            a = a + lax.dot_general(ql, kl, NT, preferred_element_type=F32) * pm_ref[l]
        o_ref[sl, :] = o + jnp.dot(a.astype(BF16), vb, preferred_element_type=F32)
        kd = (k * e_k).astype(BF16)
        st = e_last * st + lax.dot_general(vb, kd, TN, preferred_element_type=F32)
    st_scr[...] = st

    @pl.when(i == pl.num_programs(2) - 1)
    def _():
        st_ref[0, 0] = st


def _hgrn_prompt(hq, hk, hi, lf, B, L, lb_rows):
    wall, low, pm = _hgrn_consts()
    nlev = low.shape[0]
    nblk = L // lb_rows
    blk = pl.BlockSpec((lb_rows, HEAD_W), lambda b, h, i: (b * nblk + i, h))
    cst = lambda a: pl.BlockSpec(a.shape, lambda b, h, i: (0,) * a.ndim)
    return pl.pallas_call(
        functools.partial(_hgrn_kernel, nlev=nlev),
        grid=(B, HEADS, nblk),
        in_specs=[blk, blk, blk, blk, cst(wall), cst(low), cst(pm)],
        out_specs=[blk, pl.BlockSpec((1, 1, HEAD_W, HEAD_W), lambda b, h, i: (b, h, 0, 0))],
        out_shape=[jax.ShapeDtypeStruct(hq.shape, F32),
                   jax.ShapeDtypeStruct((B, HEADS, HEAD_W, HEAD_W), F32)],
        scratch_shapes=[pltpu.VMEM((HEAD_W, HEAD_W), F32)],
        compiler_params=_cparams(("parallel", "parallel", "arbitrary")),
        name="hgrn_prompt",
    )(hq, hk, hi, lf, wall, low, pm)


def _hgrn_step_kernel(qc_ref, kc_ref, lfc_ref, v_ref, s0_ref, o_ref, s_ref):
    qc, kc = qc_ref[0, 0], kc_ref[0, 0]
    dec = jnp.exp(lfc_ref[0, 0])
    v = v_ref[0, 0]
    s0 = s0_ref[0, 0]
    s_ref[0, 0] = dec * s0 + kc * v
    o_ref[0, 0] = (jnp.sum((qc * dec) * s0, axis=0, keepdims=True)
                   + jnp.sum(qc * kc, axis=0, keepdims=True) * v)


def _hgrn_step(hq, hk, hi, lf, s0):
    B = hq.shape[0]
    colv = lambda a: a.reshape(B, HEADS, HEAD_W, 1)
    cspec = pl.BlockSpec((1, 1, HEAD_W, 1), lambda b, h: (b, h, 0, 0))
    rspec = pl.BlockSpec((1, 1, 1, HEAD_W), lambda b, h: (b, h, 0, 0))
    sspec = pl.BlockSpec((1, 1, HEAD_W, HEAD_W), lambda b, h: (b, h, 0, 0))
    o, s = pl.pallas_call(
        _hgrn_step_kernel,
        grid=(B, HEADS),
        in_specs=[cspec, cspec, cspec, rspec, sspec],
        out_specs=[rspec, sspec],
        out_shape=[jax.ShapeDtypeStruct((B, HEADS, 1, HEAD_W), F32),
                   jax.ShapeDtypeStruct((B, HEADS, HEAD_W, HEAD_W), F32)],
        compiler_params=_cparams(("parallel", "parallel")),
        name="hgrn_step",
    )(colv(hq), colv(hk), colv(lf), hi.reshape(B, HEADS, 1, HEAD_W), s0)
    return o.reshape(B, GROUP_W), s


def _attn_kernel(lam_ref, q_ref, k_ref, v_ref, o_ref, qs_scr, m_scr, l_scr, acc_scr):
    qi = pl.program_id(1)
    ki = pl.program_id(2)
    bq = q_ref.shape[0]
    bk = k_ref.shape[0]

    @pl.when(ki == 0)
    def _():
        lane = lax.broadcasted_iota(I32, (bq, HEAD_W), 1)
        for h in range(HEADS):
            q = q_ref[:, h * HEAD_W:(h + 1) * HEAD_W]
            zero = jnp.zeros_like(q)
            qs_scr[h] = jnp.concatenate([jnp.where(lane < DA_DH, q, zero),
                                         jnp.where(lane >= DA_DH, q, zero)], axis=0)
        m_scr[...] = jnp.full(m_scr.shape, -jnp.inf, F32)
        l_scr[...] = jnp.zeros_like(l_scr)
        acc_scr[...] = jnp.zeros_like(acc_scr)

    @pl.when(ki <= qi)
    def _():
        row = lax.broadcasted_iota(I32, (2 * bq, bk), 0) % bq
        colv = lax.broadcasted_iota(I32, (2 * bq, bk), 1)
        keep = (colv <= row) | (ki < qi)
        for h in range(HEADS):
            kh = k_ref[:, h * HEAD_W:(h + 1) * HEAD_W]
            vh = v_ref[:, h * HEAD_W:(h + 1) * HEAD_W]
            s = lax.dot_general(qs_scr[h], kh, NT, preferred_element_type=F32)
            s = jnp.where(keep, s, -jnp.inf)
            m_old = m_scr[h]
            m_new = jnp.maximum(m_old, jnp.max(s, axis=-1, keepdims=True))
            alpha = jnp.exp(m_old - m_new)
            p = jnp.exp(s - m_new)
            l_scr[h] = alpha * l_scr[h] + jnp.sum(p, axis=-1, keepdims=True)
            acc_scr[h] = alpha * acc_scr[h] + jnp.dot(p.astype(BF16), vh, preferred_element_type=F32)
            m_scr[h] = m_new

    @pl.when(ki == qi)
    def _():
        lam = lam_ref[0]
        for h in range(HEADS):
            r = acc_scr[h] / l_scr[h]
            o_ref[:, h * HEAD_W:(h + 1) * HEAD_W] = r[:bq] - lam * r[bq:]


def _attn_prompt(lam, qb, kb, vb, B, L, bq):
    nq = L // bq
    qspec = pl.BlockSpec((bq, GROUP_W), lambda b, qi, ki: (b * nq + qi, 0))
    kspec = pl.BlockSpec((bq, GROUP_W), lambda b, qi, ki: (b * nq + jnp.minimum(ki, qi), 0))
    return pl.pallas_call(
        _attn_kernel,
        grid=(B, nq, nq),
        in_specs=[pl.BlockSpec(memory_space=pltpu.SMEM), qspec, kspec, kspec],
        out_specs=qspec,
        out_shape=jax.ShapeDtypeStruct(qb.shape, F32),
        scratch_shapes=[pltpu.VMEM((HEADS, 2 * bq, HEAD_W), BF16),
                        pltpu.VMEM((HEADS, 2 * bq, 1), F32),
                        pltpu.VMEM((HEADS, 2 * bq, 1), F32),
                        pltpu.VMEM((HEADS, 2 * bq, HEAD_W), F32)],
        compiler_params=_cparams(("parallel", "parallel", "arbitrary")),
        name="attn_prompt",
    )(lam, qb, kb, vb)


def _decode_kernel(pt_ref, lam_ref, q_ref, kn_ref, vn_ref, *rest):
    npg = PAGES_PER_STEP
    k_refs, v_refs = rest[:npg], rest[npg:2 * npg]
    o_ref, kcat, vcat, m_scr, l_scr, acc_scr = rest[2 * npg:]
    i = pl.program_id(1)
    nrow = 2 * HEADS

    rowi = lax.broadcasted_iota(I32, (nrow, GROUP_W), 0)
    lane = lax.broadcasted_iota(I32, (nrow, GROUP_W), 1)
    sel = (lane // HEAD_W == rowi // 2) & ((lane % HEAD_W) // DA_DH == rowi % 2)
    qbd = jnp.where(sel, jnp.broadcast_to(q_ref[0], (nrow, GROUP_W)), 0.0)

    @pl.when(i == 0)
    def _():
        m_scr[...] = jnp.full(m_scr.shape, -jnp.inf, F32)
        l_scr[...] = jnp.zeros_like(l_scr)
        acc_scr[...] = jnp.zeros_like(acc_scr)

    for p in range(npg):
        kcat[p * PAGE:(p + 1) * PAGE, :] = k_refs[p][0].astype(BF16)
        vcat[p * PAGE:(p + 1) * PAGE, :] = v_refs[p][0].astype(BF16)
    s = lax.dot_general(qbd.astype(BF16), kcat[...], NT, preferred_element_type=F32)
    m_old = m_scr[...]
    m_new = jnp.maximum(m_old, jnp.max(s, axis=-1, keepdims=True))
    alpha = jnp.exp(m_old - m_new)
    p_ = jnp.exp(s - m_new)
    l_scr[...] = alpha * l_scr[...] + jnp.sum(p_, axis=-1, keepdims=True)
    acc_scr[...] = alpha * acc_scr[...] + jnp.dot(p_.astype(BF16), vcat[...], preferred_element_type=F32)
    m_scr[...] = m_new

    @pl.when(i == pl.num_programs(1) - 1)
    def _():
        s_new = jnp.sum(qbd * kn_ref[0], axis=-1, keepdims=True)
        m_o = m_scr[...]
        m_n = jnp.maximum(m_o, s_new)
        al = jnp.exp(m_o - m_n)
        pn = jnp.exp(s_new - m_n)
        l_f = al * l_scr[...] + pn
        r = (al * acc_scr[...] + pn * vn_ref[0]) / l_f
        lam = lam_ref[0]
        outs = []
        for h in range(HEADS):
            blk = slice(h * HEAD_W, (h + 1) * HEAD_W)
            outs.append(r[2 * h:2 * h + 1, blk] - lam * r[2 * h + 1:2 * h + 2, blk])
        o_ref[0] = jnp.concatenate(outs, axis=-1)


def _attn_decode(page_table, lam, q, kn, vn, ck, cv):
    B = q.shape[0]
    nsteps = page_table.shape[1] // PAGES_PER_STEP
    tok = pl.BlockSpec((1, 1, GROUP_W), lambda b, i, pt: (b, 0, 0))

    def page(p):
        return pl.BlockSpec((1, PAGE, GROUP_W), lambda b, i, pt: (pt[b, i * PAGES_PER_STEP + p], 0, 0))

    pages = [page(p) for p in range(PAGES_PER_STEP)]
    grid_spec = pltpu.PrefetchScalarGridSpec(
        num_scalar_prefetch=1,
        grid=(B, nsteps),
        in_specs=[pl.BlockSpec(memory_space=pltpu.SMEM), tok, tok, tok] + pages + pages,
        out_specs=tok,
        scratch_shapes=[pltpu.VMEM((PAGES_PER_STEP * PAGE, GROUP_W), BF16),
                        pltpu.VMEM((PAGES_PER_STEP * PAGE, GROUP_W), BF16),
                        pltpu.VMEM((2 * HEADS, 1), F32),
                        pltpu.VMEM((2 * HEADS, 1), F32),
                        pltpu.VMEM((2 * HEADS, GROUP_W), F32)])
    r3 = lambda a: a.reshape(B, 1, GROUP_W)
    out = pl.pallas_call(
        _decode_kernel,
        grid_spec=grid_spec,
        out_shape=jax.ShapeDtypeStruct((B, 1, GROUP_W), F32),
        compiler_params=_cparams(("parallel", "arbitrary")),
        name="attn_decode",
    )(page_table, lam, r3(q), r3(kn), r3(vn), *([ck] * PAGES_PER_STEP), *([cv] * PAGES_PER_STEP))
    return out.reshape(B, GROUP_W)


def _mix_kernel(oh_ref, gate_ref, od_ref, x_ref, gh_ref, gd_ref, wo_ref, gf_ref, wq_ref,
                hp_ref, hn_ref, qh_ref, *, dscale):
    parts = []
    for h in range(HEADS):
        blk = slice(h * HEAD_W, (h + 1) * HEAD_W)
        parts.append(_rms(oh_ref[:, blk], gh_ref[...]) * gate_ref[:, blk])
    for h in range(HEADS):
        blk = slice(h * HEAD_W, (h + 1) * HEAD_W)
        parts.append(_rms(od_ref[:, blk], gd_ref[...]) * dscale)
    y = jnp.concatenate(parts, axis=-1).astype(BF16)
    hp = x_ref[...] + jnp.dot(y, wo_ref[...], preferred_element_type=F32)
    hp_ref[...] = hp
    hn = _rms(hp, gf_ref[...])
    hn_ref[...] = hn
    qh_ref[...] = jnp.dot(hn.astype(BF16), wq_ref[...], preferred_element_type=F32).astype(BF16)


def _mix(oh, gate, od, x, gh, gd, wo_bf, gf, wq_bf, dscale, tm):
    T = x.shape[0]
    row = lambda i: (i, 0)
    fixed = lambda i: (0, 0)
    half = pl.BlockSpec((tm, GROUP_W), row)
    full = pl.BlockSpec((tm, D_MODEL), row)
    cst = lambda a: pl.BlockSpec(a.shape, fixed)
    nq = wq_bf.shape[1]
    return pl.pallas_call(
        functools.partial(_mix_kernel, dscale=dscale),
        grid=(T // tm,),
        in_specs=[half, half, half, full, cst(gh), cst(gd), cst(wo_bf), cst(gf), cst(wq_bf)],
        out_specs=[full, full, pl.BlockSpec((tm, nq), row)],
        out_shape=[jax.ShapeDtypeStruct((T, D_MODEL), F32), jax.ShapeDtypeStruct((T, D_MODEL), F32),
                   jax.ShapeDtypeStruct((T, nq), BF16)],
        compiler_params=_cparams(("parallel",)),
        name="mix",
    )(oh, gate, od, x, gh, gd, wo_bf, gf, wq_bf)


def _staircase():
    K = PEER_TOPK
    return [(a, b) for a in range(K) for b in range(K) if (a + 1) * (b + 1) <= K]


def _topk_kernel(qh_ref, sk_ref, e_ref, g_ref, s_scr, val_scr, idx_scr, cand_scr, cidx_scr, sc_scr):
    K = PEER_TOPK
    NK = PEER_NKEYS
    tb = qh_ref.shape[0]
    half_w = PEER_HEADS * NK
    n_iota = lax.broadcasted_iota(I32, (NK, PEER_HEADS, tb), 0)
    neg = -jnp.inf

    for c in range(2):
        qc = qh_ref[:, c * half_w:(c + 1) * half_w]
        s = lax.dot_general(sk_ref[c], qc, NT, preferred_element_type=F32)
        s_scr[...] = s.reshape(NK, PEER_HEADS, tb)

        def body(a, carry):
            sv = s_scr[...]
            m = jnp.max(sv, axis=0)
            idx = jnp.min(jnp.where(sv == m[None], n_iota, NK), axis=0)
            s_scr[...] = jnp.where(n_iota == idx[None], neg, sv)
            val_scr[c, a] = m
            idx_scr[c, a] = idx
            return carry

        lax.fori_loop(0, K, body, 0)

    pairs = _staircase()
    for i, (a, b) in enumerate(pairs):
        cand_scr[i] = val_scr[0, a] + val_scr[1, b]
        cidx_scr[i] = idx_scr[0, a] * NK + idx_scr[1, b]
    flats = [a * K + b for a, b in pairs]
    big = K * K

    def body2(r, carry):
        cs = [cand_scr[i] for i in range(len(pairs))]
        m = functools.reduce(jnp.maximum, cs)
        pos = functools.reduce(jnp.minimum, [jnp.where(cv == m, fl, big) for cv, fl in zip(cs, flats)])
        e = jnp.zeros(m.shape, I32)
        for i, (cv, fl) in enumerate(zip(cs, flats)):
            hit = pos == fl
            cand_scr[i] = jnp.where(hit, neg, cv)
            e = jnp.where(hit, cidx_scr[i], e)
        sc_scr[r] = m
        e_ref[r] = e
        return carry

    lax.fori_loop(0, K, body2, 0)
    sc = sc_scr[...]
    ex = jnp.exp(sc - sc[0:1])
    g_ref[...] = ex / jnp.sum(ex, axis=0, keepdims=True)


def _topk(qh, sk_big, tb):
    T = qh.shape[0]
    ncand = len(_staircase())
    hw = (PEER_HEADS, tb)
    return pl.pallas_call(
        _topk_kernel,
        grid=(T // tb,),
        in_specs=[pl.BlockSpec((tb, qh.shape[1]), lambda i: (i, 0)),
                  pl.BlockSpec(sk_big.shape, lambda i: (0, 0, 0))],
        out_specs=[pl.BlockSpec((PEER_TOPK, PEER_HEADS, tb), lambda i: (0, 0, i))] * 2,
        out_shape=[jax.ShapeDtypeStruct((PEER_TOPK, PEER_HEADS, T), I32),
                   jax.ShapeDtypeStruct((PEER_TOPK, PEER_HEADS, T), F32)],
        scratch_shapes=[pltpu.VMEM((PEER_NKEYS,) + hw, F32),
                        pltpu.VMEM((2, PEER_TOPK) + hw, F32),
                        pltpu.VMEM((2, PEER_TOPK) + hw, I32),
                        pltpu.VMEM((ncand,) + hw, F32),
                        pltpu.VMEM((ncand,) + hw, I32),
                        pltpu.VMEM((PEER_TOPK,) + hw, F32)],
        compiler_params=_cparams(("parallel",)),
        name="peer_topk",
    )(qh, sk_big)


def _pack_table(tab):
    n = tab.shape[0]
    t = tab.astype(BF16).reshape(n * WORD_ROWS, 2, HEAD_W).transpose(0, 2, 1)
    return lax.bitcast_convert_type(t, I32)


def _gather_rows(idx_ref, tab_ref, stg, t):
    for j in range(PEER_SEL):
        r = pl.multiple_of(idx_ref[t, j] * WORD_ROWS, WORD_ROWS)
        stg[j * WORD_ROWS:(j + 1) * WORD_ROWS, :] = tab_ref[pl.ds(r, WORD_ROWS), :]


def _peer_u_kernel(idx_ref, x_ref, g_ref, tab_ref, dmask_ref, gsum_ref, expand_ref, w_ref, stg, c_scr):
    tb = x_ref.shape[0]
    dmask = dmask_ref[...]

    def group(gi, carry):
        rows = []
        for i in range(8):
            t = gi * 8 + i
            buf = stg.at[i % 2]
            _gather_rows(idx_ref, tab_ref, buf, t)
            ub = pltpu.bitcast(buf[...], BF16)
            xt = x_ref[t].astype(BF16)
            r = lax.dot_general(xt, ub, NT, preferred_element_type=F32)
            rows.append(jnp.sum(r * dmask, axis=0, keepdims=True))
        c_scr[pl.ds(pl.multiple_of(gi * 8, 8), 8), :] = jnp.concatenate(rows, axis=0)
        return carry

    lax.fori_loop(0, tb // 8, group, 0)
    c = c_scr[...]
    c_hi = c.astype(BF16)
    c_lo = (c - c_hi.astype(F32)).astype(BF16)
    a = (jnp.dot(c_hi, gsum_ref[...], preferred_element_type=F32)
         + jnp.dot(c_lo, gsum_ref[...], preferred_element_type=F32))
    w = g_ref[...] * jax.nn.gelu(a)
    w_ref[...] = jnp.dot(w.astype(BF16), expand_ref[...], preferred_element_type=F32)


def _peer_v_kernel(idx_ref, w_ref, tab_ref, dmask_ref, o_ref, stg):
    tb = w_ref.shape[0]
    dmask = dmask_ref[...]

    def group(gi, carry):
        w8 = w_ref[pl.ds(pl.multiple_of(gi * 8, 8), 8), :]
        for i in range(8):
            t = gi * 8 + i
            buf = stg.at[i % 2]
            _gather_rows(idx_ref, tab_ref, buf, t)
            vb = pltpu.bitcast(buf[...], BF16)
            lhs = (jnp.broadcast_to(w8[i:i + 1, :], dmask.shape) * dmask).astype(BF16)
            o_ref[t] = jnp.dot(lhs, vb, preferred_element_type=F32)
        return carry

    lax.fori_loop(0, tb // 8, group, 0)


def _peer_consts():
    sel_w = PEER_SEL * ROW_CHUNKS
    lane = np.arange(sel_w)
    dmask = (lane[None, :] % ROW_CHUNKS == np.arange(ROW_CHUNKS)[:, None]).astype(np.float32)
    gsum = (lane[:, None] // ROW_CHUNKS == np.arange(PEER_SEL)[None, :]).astype(np.float32)
    return jnp.asarray(dmask, F32), jnp.asarray(gsum, BF16), jnp.asarray(gsum.T, BF16)


def _peer_u(idx, hn, g, tab_u, tb):
    T = hn.shape[0]
    dmask, gsum, expand = _peer_consts()
    sel_w = PEER_SEL * ROW_CHUNKS
    cst = lambda a: pl.BlockSpec(a.shape, lambda i: (0, 0))
    return pl.pallas_call(
        _peer_u_kernel,
        grid=(T // tb,),
        in_specs=[pl.BlockSpec((tb, PEER_SEL), lambda i: (i, 0), memory_space=pltpu.SMEM),
                  pl.BlockSpec((tb, ROW_CHUNKS, HEAD_W), lambda i: (i, 0, 0)),
                  pl.BlockSpec((tb, PEER_SEL), lambda i: (i, 0)),
                  pl.BlockSpec(memory_space=pltpu.VMEM),
                  cst(dmask), cst(gsum), cst(expand)],
        out_specs=pl.BlockSpec((tb, sel_w), lambda i: (i, 0)),
        out_shape=jax.ShapeDtypeStruct((T, sel_w), F32),
        scratch_shapes=[pltpu.VMEM((2, PEER_SEL * WORD_ROWS, HEAD_W), I32),
                        pltpu.VMEM((tb, sel_w), F32)],
        compiler_params=_cparams(("arbitrary",)),
        name="peer_u",
    )(idx, hn.reshape(T, ROW_CHUNKS, HEAD_W), g, tab_u, dmask, gsum, expand)


def _peer_v(idx, wexp, tab_v, tb):
    T = wexp.shape[0]
    dmask, _, _ = _peer_consts()
    out = pl.pallas_call(
        _peer_v_kernel,
        grid=(T // tb,),
        in_specs=[pl.BlockSpec((tb, PEER_SEL), lambda i: (i, 0), memory_space=pltpu.SMEM),
                  pl.BlockSpec((tb, wexp.shape[1]), lambda i: (i, 0)),
                  pl.BlockSpec(memory_space=pltpu.VMEM),
                  pl.BlockSpec(dmask.shape, lambda i: (0, 0))],
        out_specs=pl.BlockSpec((tb, ROW_CHUNKS, HEAD_W), lambda i: (i, 0, 0)),
        out_shape=jax.ShapeDtypeStruct((T, ROW_CHUNKS, HEAD_W), F32),
        scratch_shapes=[pltpu.VMEM((2, PEER_SEL * WORD_ROWS, HEAD_W), I32)],
        compiler_params=_cparams(("arbitrary",)),
        name="peer_v",
    )(idx, wexp, tab_v, dmask)
    return out.reshape(T, D_MODEL)


def _ple_kernel(hp_ref, pe_ref, p_ref, wg_ref, bg_ref, wp_ref, gf_ref, y_ref):
    h = hp_ref[...] + pe_ref[...]
    gate = jax.nn.sigmoid(jnp.dot(h.astype(BF16), wg_ref[...], preferred_element_type=F32) + bg_ref[...])
    h = h + gate * jnp.dot(p_ref[...].astype(BF16), wp_ref[...], preferred_element_type=F32)
    y_ref[...] = _rms(h, gf_ref[...])


def _ple(hp, pe, p, wg_bf, bg, wp_bf, gf, tm):
    T = hp.shape[0]
    row = lambda i: (i, 0)
    full = pl.BlockSpec((tm, D_MODEL), row)
    cst = lambda a: pl.BlockSpec(a.shape, lambda i: (0, 0))
    return pl.pallas_call(
        _ple_kernel,
        grid=(T // tm,),
        in_specs=[full, full, pl.BlockSpec((tm, PLE_DIM), row), cst(wg_bf), cst(bg), cst(wp_bf), cst(gf)],
        out_specs=full,
        out_shape=jax.ShapeDtypeStruct((T, D_MODEL), F32),
        compiler_params=_cparams(("parallel",)),
        name="ple",
    )(hp, pe, p, wg_bf, bg, wp_bf, gf)


def _rope_tables(pos):
    half = DA_DH // 2
    freqs = ROPE_THETA ** (-jnp.arange(half, dtype=F32) / half)
    ang = pos.astype(F32)[:, None] * freqs[None, :]
    cos = jnp.tile(jnp.cos(ang), (1, GROUP_W // half))
    sign = jnp.where((jnp.arange(GROUP_W) % DA_DH) < half, -1.0, 1.0).astype(F32)
    sin = jnp.tile(jnp.sin(ang), (1, GROUP_W // half)) * sign[None, :]
    return cos, sin


def _ffn(hp_parts, p, weights, tb_peer):
    hp, hn, qh = hp_parts
    (sk_big, tab_u, tab_v, wg_bf, bg, wp_bf, gfinal) = weights
    T = hp.shape[0]
    tk = 128
    tpad = -(-T // tk) * tk
    qh_p = jnp.pad(qh, ((0, tpad - T), (0, 0))) if tpad != T else qh
    e, g = _topk(qh_p, sk_big, tk)
    idx = e.reshape(PEER_SEL, tpad).T[:T]
    gw = g.reshape(PEER_SEL, tpad).T[:T]
    wexp = _peer_u(idx, hn, gw, tab_u, tb_peer)
    pe = _peer_v(idx, wexp, tab_v, tb_peer)
    tm = min(256, T)
    return _ple(hp, pe, p, wg_bf, bg, wp_bf, gfinal, tm)


def kernel(x_prompt, x_sample, p_prompt, p_sample, cache_k, cache_v, state_hgrn, page_table, g_attn, w_in, hgrn_gamma, g_hgrn_norm, lambda_q1, lambda_k1, lambda_q2, lambda_k2, g_diff_norm, w_out, g_ffn, peer_w_query, peer_sub_keys, peer_u, peer_v, ple_w_gate, ple_b_gate, ple_w_proj, g_final):
    Bp, Lp, D = x_prompt.shape
    Bs, Ls, _ = x_sample.shape
    assert D == D_MODEL and Ls == 1 and w_in.shape[0] == 1
    l = 0
    past_len = page_table.shape[1] * cache_k.shape[2]
    lam_init = 0.8 - 0.6 * math.exp(-0.3 * l)
    lam = (jnp.exp(jnp.sum(lambda_q1[l] * lambda_k1[l])) - jnp.exp(jnp.sum(lambda_q2[l] * lambda_k2[l]))
           + lam_init).reshape(1).astype(F32)
    lb = jnp.cumsum(jax.nn.softmax(hgrn_gamma.astype(F32), axis=0), axis=0)[l].reshape(1, GROUP_W)

    w_in_bf = w_in[l].astype(BF16)
    w_out_bf = w_out[l].astype(BF16)
    nqc = PEER_HEADS * 2 * PEER_NKEYS
    wq_bf = (peer_w_query[l].reshape(D, PEER_HEADS, 2, PEER_NKEYS).transpose(0, 2, 1, 3)
             .reshape(D, nqc).astype(BF16))
    sk_big = jnp.einsum('hcnk,hg->cnhgk', peer_sub_keys[l], jnp.eye(PEER_HEADS, dtype=F32)).reshape(
        2, PEER_NKEYS * PEER_HEADS, PEER_HEADS * PEER_NKEYS).astype(BF16)
    tab_u = _pack_table(peer_u[l])
    tab_v = _pack_table(peer_v[l])
    wg_bf = ple_w_gate[l].astype(BF16)
    wp_bf = ple_w_proj[l].astype(BF16)
    row = lambda a: a.reshape(1, -1).astype(F32)
    ffn_w = (sk_big, tab_u, tab_v, wg_bf, row(ple_b_gate[l]), wp_bf, row(g_final))

    def group(x, pos_tab, tm):
        cosf, sins = pos_tab
        return _proj(x, row(g_attn[l]), w_in_bf, lb, cosf, sins, tm)

    def mix(oh, gate, od, x, tm):
        return _mix(oh, gate, od, x, row(g_hgrn_norm[l]), row(g_diff_norm[l]), w_out_bf, row(g_ffn[l]),
                    wq_bf, 1.0 - lam_init, tm)

    Tp = Bp * Lp
    xp = x_prompt.reshape(Tp, D)
    tm_p = math.gcd(Lp, 512)
    hq, hk, lf, hi, gate, k_p, v_p, qb, kb, vb = group(xp, _rope_tables(jnp.arange(Lp)), tm_p)
    o_h, st_p = _hgrn_prompt(hq, hk, hi, lf, Bp, Lp, math.gcd(Lp, 512))
    o_d = _attn_prompt(lam, qb, kb, vb, Bp, Lp, math.gcd(Lp, 512))
    y_p = _ffn(mix(o_h, gate, o_d, xp, min(256, Tp)), p_prompt[l].reshape(Tp, PLE_DIM), ffn_w, min(64, Tp))

    xs = x_sample.reshape(Bs, D)
    pos_s = jnp.full((Bs,), past_len, dtype=jnp.int32)
    hq, hk, lf, hi, gate, k_s, v_s, qb, kb, vb = group(xs, _rope_tables(pos_s), Bs)
    o_h, st_s = _hgrn_step(hq, hk, hi, lf, state_hgrn[l])
    npool = cache_k.shape[1]
    o_d = _attn_decode(page_table, lam, qb.astype(F32), k_s, v_s,
                       cache_k[l].reshape(npool, PAGE, GROUP_W), cache_v[l].reshape(npool, PAGE, GROUP_W))
    y_s = _ffn(mix(o_h, gate, o_d, xs, Bs), p_sample[l].reshape(Bs, PLE_DIM), ffn_w, Bs)

    hd = (HEADS, HEAD_W)
    return (y_p.reshape(Bp, Lp, D), y_s.reshape(Bs, Ls, D),
            k_p.reshape((1, Bp, Lp) + hd), v_p.reshape((1, Bp, Lp) + hd),
            jnp.swapaxes(st_p, -1, -2)[None],
            k_s.reshape((1, Bs, Ls) + hd), v_s.reshape((1, Bs, Ls) + hd), st_s[None])
```

```python
import functools
import math

import numpy as np
import jax
import jax.numpy as jnp
from jax import lax
from jax.experimental import pallas as pl
from jax.experimental.pallas import tpu as pltpu

F32 = jnp.float32
BF16 = jnp.bfloat16
I32 = jnp.int32

D_MODEL = 1024
HEADS = 4
HEAD_W = 128
GROUP_W = HEADS * HEAD_W
DA_DH = 64
ROPE_THETA = 10000.0
HG_CHUNK = 64
PEER_HEADS = 8
PEER_NKEYS = 128
PEER_TOPK = 16
PEER_SEL = PEER_HEADS * PEER_TOPK
PLE_DIM = 256
EPS = 1e-6
PAGE = 128
PAGES_PER_STEP = 8
ROW_CHUNKS = D_MODEL // HEAD_W
WORD_ROWS = ROW_CHUNKS // 2
VMEM_LIMIT = 56 * 1024 * 1024

NT = (((1,), (1,)), ((), ()))
TN = (((0,), (0,)), ((), ()))


def _cparams(sem):
    return pltpu.CompilerParams(dimension_semantics=sem, vmem_limit_bytes=VMEM_LIMIT)


def _rms(x, g):
    return x * lax.rsqrt(jnp.mean(x * x, axis=-1, keepdims=True) + EPS) * g


def _proj_kernel(x_ref, g_ref, w_ref, lb_ref, cos_ref, sin_ref,
                 hq_ref, hk_ref, lf_ref, hi_ref, gate_ref, k_ref, v_ref,
                 qb_ref, kb_ref, vb_ref):
    xb = _rms(x_ref[...], g_ref[...]).astype(BF16)

    def col(i):
        return jnp.dot(xb, w_ref[:, i * GROUP_W:(i + 1) * GROUP_W], preferred_element_type=F32)

    hq_ref[...] = col(0)
    lb = lb_ref[...]
    f = lb + (1.0 - lb) * jax.nn.sigmoid(col(1))
    lf_ref[...] = jnp.log(f)
    hk_ref[...] = 1.0 - f
    hi_ref[...] = col(2)
    gate_ref[...] = jax.nn.silu(col(3))

    cosf = cos_ref[...]
    sins = sin_ref[...]
    lane = lax.broadcasted_iota(I32, cosf.shape, 1)
    first_half = (lane % DA_DH) < (DA_DH // 2)

    def rope(x):
        swapped = jnp.where(first_half,
                            pltpu.roll(x, GROUP_W - DA_DH // 2, 1),
                            pltpu.roll(x, DA_DH // 2, 1))
        return x * cosf + swapped * sins

    q = rope(col(4))
    qb_ref[...] = (q * (DA_DH ** -0.5)).astype(BF16)
    k = rope(col(5))
    k_ref[...] = k
    kb_ref[...] = k.astype(BF16)
    v = col(6)
    v_ref[...] = v
    vb_ref[...] = v.astype(BF16)


def _proj(x, g, w_bf, lb, cosf, sins, tm):
    T = x.shape[0]
    nl = cosf.shape[0] // tm
    row = lambda i: (i, 0)
    fixed = lambda i: (0, 0)
    tab = lambda i: (i % nl, 0)
    f32o = jax.ShapeDtypeStruct((T, GROUP_W), F32)
    bfo = jax.ShapeDtypeStruct((T, GROUP_W), BF16)
    ospec = pl.BlockSpec((tm, GROUP_W), row)
    return pl.pallas_call(
        _proj_kernel,
        grid=(T // tm,),
        in_specs=[pl.BlockSpec((tm, D_MODEL), row),
                  pl.BlockSpec((1, D_MODEL), fixed),
                  pl.BlockSpec(w_bf.shape, fixed),
                  pl.BlockSpec((1, GROUP_W), fixed),
                  pl.BlockSpec((tm, GROUP_W), tab),
                  pl.BlockSpec((tm, GROUP_W), tab)],
        out_specs=[ospec] * 10,
        out_shape=[f32o] * 7 + [bfo] * 3,
        compiler_params=_cparams(("parallel",)),
        name="proj",
    )(x, g, w_bf, lb, cosf, sins)


def _hgrn_consts():
    C = HG_CHUNK
    t = np.arange(C)[:, None]
    u = np.arange(C)[None, :]
    mats = [(u <= t)]
    lows, pms = [], []
    h = C // 2
    while h >= 1:
        base = (t // (2 * h)) * (2 * h)
        r = base + h - 1
        lower = t >= base + h
        m = np.where(lower, (u > r) & (u <= t), (u > t) & (u <= r))
        mats.append(m)
        lows.append(np.broadcast_to(lower, (C, HEAD_W)))
        s = np.arange(C)[None, :]
        pms.append((t // (2 * h)) == (s // (2 * h)))
        h //= 2
    mats.append(u > t)
    wall = np.concatenate(mats, axis=0).astype(np.float32)
    return (jnp.asarray(wall, BF16), jnp.asarray(np.stack(lows), F32), jnp.asarray(np.stack(pms), F32))


def _split3(x):
    a = x.astype(BF16)
    r = x - a.astype(F32)
    b = r.astype(BF16)
    c = (r - b.astype(F32)).astype(BF16)
    return a, b, c


def _hgrn_kernel(q_ref, k_ref, v_ref, lf_ref, wall_ref, low_ref, pm_ref, o_ref, st_ref, st_scr, *, nlev):
    i = pl.program_id(2)
    C = HG_CHUNK

    @pl.when(i == 0)
    def _():
        st_scr[...] = jnp.zeros_like(st_scr)

    wall = wall_ref[...]
    eye = (lax.broadcasted_iota(I32, (C, C), 0) == lax.broadcasted_iota(I32, (C, C), 1)).astype(F32)
    st = st_scr[...]
    for c in range(q_ref.shape[0] // C):
        sl = slice(c * C, (c + 1) * C)
        q, k, v, lf = q_ref[sl, :], k_ref[sl, :], v_ref[sl, :], lf_ref[sl, :]
        lcat = jnp.concatenate(_split3(lf), axis=-1)
        d3 = jnp.dot(wall, lcat, preferred_element_type=F32)
        e_all = jnp.exp(d3[:, :HEAD_W] + d3[:, HEAD_W:2 * HEAD_W] + d3[:, 2 * HEAD_W:])
        e_g = e_all[0:C]
        e_last = e_all[C - 1:C]
        e_k = e_all[(nlev + 1) * C:(nlev + 2) * C]
        vb = v.astype(BF16)
        o = lax.dot_general((q * e_g).astype(BF16), st.astype(BF16), NT, preferred_element_type=F32)
        a = eye * jnp.sum(q * k, axis=-1, keepdims=True)
        for l in range(nlev):
            e_l = e_all[(l + 1) * C:(l + 2) * C]
            low = low_ref[l]
            ql = (q * e_l * low).astype(BF16)
            kl = (k * e_l * (1.0 - low)).astype(BF16)
            a = a + lax.dot_general(ql, kl, NT, preferred_element_type=F32) * pm_ref[l]
        o_ref[sl, :] = o + jnp.dot(a.astype(BF16), vb, preferred_element_type=F32)
        kd = (k * e_k).astype(BF16)
        st = e_last * st + lax.dot_general(vb, kd, TN, preferred_element_type=F32)
    st_scr[...] = st

    @pl.when(i == pl.num_programs(2) - 1)
    def _():
        st_ref[0, 0] = st


def _hgrn_prompt(hq, hk, hi, lf, B, L, lb_rows):
    wall, low, pm = _hgrn_consts()
    nlev = low.shape[0]
    nblk = L // lb_rows
    blk = pl.BlockSpec((lb_rows, HEAD_W), lambda b, h, i: (b * nblk + i, h))
    cst = lambda a: pl.BlockSpec(a.shape, lambda b, h, i: (0,) * a.ndim)
    return pl.pallas_call(
        functools.partial(_hgrn_kernel, nlev=nlev),
        grid=(B, HEADS, nblk),
        in_specs=[blk, blk, blk, blk, cst(wall), cst(low), cst(pm)],
        out_specs=[blk, pl.BlockSpec((1, 1, HEAD_W, HEAD_W), lambda b, h, i: (b, h, 0, 0))],
        out_shape=[jax.ShapeDtypeStruct(hq.shape, F32),
                   jax.ShapeDtypeStruct((B, HEADS, HEAD_W, HEAD_W), F32)],
        scratch_shapes=[pltpu.VMEM((HEAD_W, HEAD_W), F32)],
        compiler_params=_cparams(("parallel", "parallel", "arbitrary")),
        name="hgrn_prompt",
    )(hq, hk, hi, lf, wall, low, pm)


def _hgrn_step_kernel(qc_ref, kc_ref, lfc_ref, v_ref, s0_ref, o_ref, s_ref):
    qc, kc = qc_ref[0, 0], kc_ref[0, 0]
    dec = jnp.exp(lfc_ref[0, 0])
    v = v_ref[0, 0]
    s0 = s0_ref[0, 0]
    s_ref[0, 0] = dec * s0 + kc * v
    o_ref[0, 0] = (jnp.sum((qc * dec) * s0, axis=0, keepdims=True)
                   + jnp.sum(qc * kc, axis=0, keepdims=True) * v)


def _hgrn_step(hq, hk, hi, lf, s0):
    B = hq.shape[0]
    colv = lambda a: a.reshape(B, HEADS, HEAD_W, 1)
    cspec = pl.BlockSpec((1, 1, HEAD_W, 1), lambda b, h: (b, h, 0, 0))
    rspec = pl.BlockSpec((1, 1, 1, HEAD_W), lambda b, h: (b, h, 0, 0))
    sspec = pl.BlockSpec((1, 1, HEAD_W, HEAD_W), lambda b, h: (b, h, 0, 0))
    o, s = pl.pallas_call(
        _hgrn_step_kernel,
        grid=(B, HEADS),
        in_specs=[cspec, cspec, cspec, rspec, sspec],
        out_specs=[rspec, sspec],
        out_shape=[jax.ShapeDtypeStruct((B, HEADS, 1, HEAD_W), F32),
                   jax.ShapeDtypeStruct((B, HEADS, HEAD_W, HEAD_W), F32)],
        compiler_params=_cparams(("parallel", "parallel")),
        name="hgrn_step",
    )(colv(hq), colv(hk), colv(lf), hi.reshape(B, HEADS, 1, HEAD_W), s0)
    return o.reshape(B, GROUP_W), s


def _attn_kernel(lam_ref, q_ref, k_ref, v_ref, o_ref, qs_scr, m_scr, l_scr, acc_scr):
    qi = pl.program_id(1)
    ki = pl.program_id(2)
    bq = q_ref.shape[0]
    bk = k_ref.shape[0]

    @pl.when(ki == 0)
    def _():
        lane = lax.broadcasted_iota(I32, (bq, HEAD_W), 1)
        for h in range(HEADS):
            q = q_ref[:, h * HEAD_W:(h + 1) * HEAD_W]
            zero = jnp.zeros_like(q)
            qs_scr[h] = jnp.concatenate([jnp.where(lane < DA_DH, q, zero),
                                         jnp.where(lane >= DA_DH, q, zero)], axis=0)
        m_scr[...] = jnp.full(m_scr.shape, -jnp.inf, F32)
        l_scr[...] = jnp.zeros_like(l_scr)
        acc_scr[...] = jnp.zeros_like(acc_scr)

    @pl.when(ki <= qi)
    def _():
        row = lax.broadcasted_iota(I32, (2 * bq, bk), 0) % bq
        colv = lax.broadcasted_iota(I32, (2 * bq, bk), 1)
        keep = (colv <= row) | (ki < qi)
        for h in range(HEADS):
            kh = k_ref[:, h * HEAD_W:(h + 1) * HEAD_W]
            vh = v_ref[:, h * HEAD_W:(h + 1) * HEAD_W]
            s = lax.dot_general(qs_scr[h], kh, NT, preferred_element_type=F32)
            s = jnp.where(keep, s, -jnp.inf)
            m_old = m_scr[h]
            m_new = jnp.maximum(m_old, jnp.max(s, axis=-1, keepdims=True))
            alpha = jnp.exp(m_old - m_new)
            p = jnp.exp(s - m_new)
            l_scr[h] = alpha * l_scr[h] + jnp.sum(p, axis=-1, keepdims=True)
            acc_scr[h] = alpha * acc_scr[h] + jnp.dot(p.astype(BF16), vh, preferred_element_type=F32)
            m_scr[h] = m_new

    @pl.when(ki == qi)
    def _():
        lam = lam_ref[0]
        for h in range(HEADS):
            r = acc_scr[h] / l_scr[h]
            o_ref[:, h * HEAD_W:(h + 1) * HEAD_W] = r[:bq] - lam * r[bq:]


def _attn_prompt(lam, qb, kb, vb, B, L, bq):
    nq = L // bq
    qspec = pl.BlockSpec((bq, GROUP_W), lambda b, qi, ki: (b * nq + qi, 0))
    kspec = pl.BlockSpec((bq, GROUP_W), lambda b, qi, ki: (b * nq + jnp.minimum(ki, qi), 0))
    return pl.pallas_call(
        _attn_kernel,
        grid=(B, nq, nq),
        in_specs=[pl.BlockSpec(memory_space=pltpu.SMEM), qspec, kspec, kspec],
        out_specs=qspec,
        out_shape=jax.ShapeDtypeStruct(qb.shape, F32),
        scratch_shapes=[pltpu.VMEM((HEADS, 2 * bq, HEAD_W), BF16),
                        pltpu.VMEM((HEADS, 2 * bq, 1), F32),
                        pltpu.VMEM((HEADS, 2 * bq, 1), F32),
                        pltpu.VMEM((HEADS, 2 * bq, HEAD_W), F32)],
        compiler_params=_cparams(("parallel", "parallel", "arbitrary")),
        name="attn_prompt",
    )(lam, qb, kb, vb)


def _decode_kernel(pt_ref, lam_ref, q_ref, kn_ref, vn_ref, *rest):
    npg = PAGES_PER_STEP
    k_refs, v_refs = rest[:npg], rest[npg:2 * npg]
    o_ref, kcat, vcat, m_scr, l_scr, acc_scr = rest[2 * npg:]
    i = pl.program_id(1)
    nrow = 2 * HEADS

    rowi = lax.broadcasted_iota(I32, (nrow, GROUP_W), 0)
    lane = lax.broadcasted_iota(I32, (nrow, GROUP_W), 1)
    sel = (lane // HEAD_W == rowi // 2) & ((lane % HEAD_W) // DA_DH == rowi % 2)
    qbd = jnp.where(sel, jnp.broadcast_to(q_ref[0], (nrow, GROUP_W)), 0.0)

    @pl.when(i == 0)
    def _():
        m_scr[...] = jnp.full(m_scr.shape, -jnp.inf, F32)
        l_scr[...] = jnp.zeros_like(l_scr)
        acc_scr[...] = jnp.zeros_like(acc_scr)

    for p in range(npg):
        kcat[p * PAGE:(p + 1) * PAGE, :] = k_refs[p][0].astype(BF16)
        vcat[p * PAGE:(p + 1) * PAGE, :] = v_refs[p][0].astype(BF16)
    s = lax.dot_general(qbd.astype(BF16), kcat[...], NT, preferred_element_type=F32)
    m_old = m_scr[...]
    m_new = jnp.maximum(m_old, jnp.max(s, axis=-1, keepdims=True))
    alpha = jnp.exp(m_old - m_new)
    p_ = jnp.exp(s - m_new)
    l_scr[...] = alpha * l_scr[...] + jnp.sum(p_, axis=-1, keepdims=True)
    acc_scr[...] = alpha * acc_scr[...] + jnp.dot(p_.astype(BF16), vcat[...], preferred_element_type=F32)
    m_scr[...] = m_new

    @pl.when(i == pl.num_programs(1) - 1)
    def _():
        s_new = jnp.sum(qbd * kn_ref[0], axis=-1, keepdims=True)
        m_o = m_scr[...]
        m_n = jnp.maximum(m_o, s_new)
        al = jnp.exp(m_o - m_n)
        pn = jnp.exp(s_new - m_n)
        l_f = al * l_scr[...] + pn
        r = (al * acc_scr[...] + pn * vn_ref[0]) / l_f
        lam = lam_ref[0]
        outs = []
        for h in range(HEADS):
            blk = slice(h * HEAD_W, (h + 1) * HEAD_W)
            outs.append(r[2 * h:2 * h + 1, blk] - lam * r[2 * h + 1:2 * h + 2, blk])
        o_ref[0] = jnp.concatenate(outs, axis=-1)


def _attn_decode(page_table, lam, q, kn, vn, ck, cv):
    B = q.shape[0]
    nsteps = page_table.shape[1] // PAGES_PER_STEP
    tok = pl.BlockSpec((1, 1, GROUP_W), lambda b, i, pt: (b, 0, 0))

    def page(p):
        return pl.BlockSpec((1, PAGE, GROUP_W), lambda b, i, pt: (pt[b, i * PAGES_PER_STEP + p], 0, 0))

    pages = [page(p) for p in range(PAGES_PER_STEP)]
    grid_spec = pltpu.PrefetchScalarGridSpec(
        num_scalar_prefetch=1,
        grid=(B, nsteps),
        in_specs=[pl.BlockSpec(memory_space=pltpu.SMEM), tok, tok, tok] + pages + pages,
        out_specs=tok,
        scratch_shapes=[pltpu.VMEM((PAGES_PER_STEP * PAGE, GROUP_W), BF16),
                        pltpu.VMEM((PAGES_PER_STEP * PAGE, GROUP_W), BF16),
                        pltpu.VMEM((2 * HEADS, 1), F32),
                        pltpu.VMEM((2 * HEADS, 1), F32),
                        pltpu.VMEM((2 * HEADS, GROUP_W), F32)])
    r3 = lambda a: a.reshape(B, 1, GROUP_W)
    out = pl.pallas_call(
        _decode_kernel,
        grid_spec=grid_spec,
        out_shape=jax.ShapeDtypeStruct((B, 1, GROUP_W), F32),
        compiler_params=_cparams(("parallel", "arbitrary")),
        name="attn_decode",
    )(page_table, lam, r3(q), r3(kn), r3(vn), *([ck] * PAGES_PER_STEP), *([cv] * PAGES_PER_STEP))
    return out.reshape(B, GROUP_W)


def _mix_kernel(oh_ref, gate_ref, od_ref, x_ref, gh_ref, gd_ref, wo_ref, gf_ref, wq_ref,
                hp_ref, hn_ref, qh_ref, *, dscale):
    parts = []
    for h in range(HEADS):
        blk = slice(h * HEAD_W, (h + 1) * HEAD_W)
        parts.append(_rms(oh_ref[:, blk], gh_ref[...]) * gate_ref[:, blk])
    for h in range(HEADS):
        blk = slice(h * HEAD_W, (h + 1) * HEAD_W)
        parts.append(_rms(od_ref[:, blk], gd_ref[...]) * dscale)
    y = jnp.concatenate(parts, axis=-1).astype(BF16)
    hp = x_ref[...] + jnp.dot(y, wo_ref[...], preferred_element_type=F32)
    hp_ref[...] = hp
    hn = _rms(hp, gf_ref[...])
    hn_ref[...] = hn
    qh_ref[...] = jnp.dot(hn.astype(BF16), wq_ref[...], preferred_element_type=F32).astype(BF16)


def _mix(oh, gate, od, x, gh, gd, wo_bf, gf, wq_bf, dscale, tm):
    T = x.shape[0]
    row = lambda i: (i, 0)
    fixed = lambda i: (0, 0)
    half = pl.BlockSpec((tm, GROUP_W), row)
    full = pl.BlockSpec((tm, D_MODEL), row)
    cst = lambda a: pl.BlockSpec(a.shape, fixed)
    nq = wq_bf.shape[1]
    return pl.pallas_call(
        functools.partial(_mix_kernel, dscale=dscale),
        grid=(T // tm,),
        in_specs=[half, half, half, full, cst(gh), cst(gd), cst(wo_bf), cst(gf), cst(wq_bf)],
        out_specs=[full, full, pl.BlockSpec((tm, nq), row)],
        out_shape=[jax.ShapeDtypeStruct((T, D_MODEL), F32), jax.ShapeDtypeStruct((T, D_MODEL), F32),
                   jax.ShapeDtypeStruct((T, nq), BF16)],
        compiler_params=_cparams(("parallel",)),
        name="mix",
    )(oh, gate, od, x, gh, gd, wo_bf, gf, wq_bf)


def _staircase():
    K = PEER_TOPK
    return [(a, b) for a in range(K) for b in range(K) if (a + 1) * (b + 1) <= K]


def _topk_kernel(qh_ref, sk_ref, e_ref, g_ref, s_scr, val_scr, idx_scr, cand_scr, cidx_scr, sc_scr, e_scr):
    K = PEER_TOPK
    NK = PEER_NKEYS
    tb = qh_ref.shape[0]
    half_w = PEER_HEADS * NK
    n_iota = lax.broadcasted_iota(I32, (NK, PEER_HEADS, tb), 0)
    neg = -jnp.inf

    for c in range(2):
        qc = qh_ref[:, c * half_w:(c + 1) * half_w]
        s = lax.dot_general(sk_ref[c], qc, NT, preferred_element_type=F32)
        s_scr[...] = s.reshape(NK, PEER_HEADS, tb)

        def body(a, carry):
            sv = s_scr[...]
            m = jnp.max(sv, axis=0)
            idx = jnp.min(jnp.where(sv == m[None], n_iota, NK), axis=0)
            s_scr[...] = jnp.where(n_iota == idx[None], neg, sv)
            val_scr[c, a] = m
            idx_scr[c, a] = idx
            return carry

        lax.fori_loop(0, K, body, 0)

    pairs = _staircase()
    for i, (a, b) in enumerate(pairs):
        cand_scr[i] = val_scr[0, a] + val_scr[1, b]
        cidx_scr[i] = (idx_scr[0, a] * NK + idx_scr[1, b]) * WORD_ROWS
    flats = [a * K + b for a, b in pairs]
    big = K * K

    def body2(r, carry):
        cs = [cand_scr[i] for i in range(len(pairs))]
        m = functools.reduce(jnp.maximum, cs)
        pos = functools.reduce(jnp.minimum, [jnp.where(cv == m, fl, big) for cv, fl in zip(cs, flats)])
        e = jnp.zeros(m.shape, I32)
        for i, (cv, fl) in enumerate(zip(cs, flats)):
            hit = pos == fl
            cand_scr[i] = jnp.where(hit, neg, cv)
            e = jnp.where(hit, cidx_scr[i], e)
        sc_scr[r] = m
        e_scr[r] = e
        return carry

    lax.fori_loop(0, K, body2, 0)
    sc = sc_scr[...]
    ex = jnp.exp(sc - sc[0:1])
    g = ex / jnp.sum(ex, axis=0, keepdims=True)
    g_ref[...] = g.reshape(PEER_SEL, tb).T
    e_ref[...] = e_scr[...].reshape(PEER_SEL, tb).T


def _topk(qh, sk_big, tb):
    T = qh.shape[0]
    ncand = len(_staircase())
    hw = (PEER_HEADS, tb)
    return pl.pallas_call(
        _topk_kernel,
        grid=(T // tb,),
        in_specs=[pl.BlockSpec((tb, qh.shape[1]), lambda i: (i, 0)),
                  pl.BlockSpec(sk_big.shape, lambda i: (0, 0, 0))],
        out_specs=[pl.BlockSpec((tb, PEER_SEL), lambda i: (i, 0))] * 2,
        out_shape=[jax.ShapeDtypeStruct((T, PEER_SEL), I32),
                   jax.ShapeDtypeStruct((T, PEER_SEL), F32)],
        scratch_shapes=[pltpu.VMEM((PEER_NKEYS,) + hw, F32),
                        pltpu.VMEM((2, PEER_TOPK) + hw, F32),
                        pltpu.VMEM((2, PEER_TOPK) + hw, I32),
                        pltpu.VMEM((ncand,) + hw, F32),
                        pltpu.VMEM((ncand,) + hw, I32),
                        pltpu.VMEM((PEER_TOPK,) + hw, F32),
                        pltpu.VMEM((PEER_TOPK,) + hw, I32)],
        compiler_params=_cparams(("parallel",)),
        name="peer_topk",
    )(qh, sk_big)


def _pack_table(tab):
    n = tab.shape[0]
    t = tab.astype(BF16).reshape(n * WORD_ROWS, 2, HEAD_W).transpose(0, 2, 1)
    return lax.bitcast_convert_type(t, I32)


def _gather_rows(idx_ref, tab_ref, stg, t):
    for j in range(PEER_SEL):
        r = pl.multiple_of(idx_ref[t, j], WORD_ROWS)
        stg[j * WORD_ROWS:(j + 1) * WORD_ROWS, :] = tab_ref[pl.ds(r, WORD_ROWS), :]


def _peer_u_kernel(idx_ref, x_ref, g_ref, tab_ref, dmask_ref, gsum_ref, expand_ref, w_ref, stg, c_scr):
    tb = x_ref.shape[0]
    dmask = dmask_ref[...]

    def group(gi, carry):
        rows = []
        for i in range(8):
            t = gi * 8 + i
            buf = stg.at[i % 2]
            _gather_rows(idx_ref, tab_ref, buf, t)
            ub = pltpu.bitcast(buf[...], BF16)
            xt = x_ref[t].astype(BF16)
            r = lax.dot_general(xt, ub, NT, preferred_element_type=F32)
            rows.append(jnp.sum(r * dmask, axis=0, keepdims=True))
        c_scr[pl.ds(pl.multiple_of(gi * 8, 8), 8), :] = jnp.concatenate(rows, axis=0)
        return carry

    lax.fori_loop(0, tb // 8, group, 0)
    c = c_scr[...]
    c_hi = c.astype(BF16)
    c_lo = (c - c_hi.astype(F32)).astype(BF16)
    a = (jnp.dot(c_hi, gsum_ref[...], preferred_element_type=F32)
         + jnp.dot(c_lo, gsum_ref[...], preferred_element_type=F32))
    w = g_ref[...] * jax.nn.gelu(a)
    w_ref[...] = jnp.dot(w.astype(BF16), expand_ref[...], preferred_element_type=F32)


def _peer_v_kernel(idx_ref, w_ref, tab_ref, dmask_ref, o_ref, stg):
    tb = w_ref.shape[0]
    dmask = dmask_ref[...]

    def group(gi, carry):
        w8 = w_ref[pl.ds(pl.multiple_of(gi * 8, 8), 8), :]
        for i in range(8):
            t = gi * 8 + i
            buf = stg.at[i % 2]
            _gather_rows(idx_ref, tab_ref, buf, t)
            vb = pltpu.bitcast(buf[...], BF16)
            lhs = (jnp.broadcast_to(w8[i:i + 1, :], dmask.shape) * dmask).astype(BF16)
            o_ref[t] = jnp.dot(lhs, vb, preferred_element_type=F32)
        return carry

    lax.fori_loop(0, tb // 8, group, 0)


def _peer_consts():
    sel_w = PEER_SEL * ROW_CHUNKS
    lane = np.arange(sel_w)
    dmask = (lane[None, :] % ROW_CHUNKS == np.arange(ROW_CHUNKS)[:, None]).astype(np.float32)
    gsum = (lane[:, None] // ROW_CHUNKS == np.arange(PEER_SEL)[None, :]).astype(np.float32)
    return jnp.asarray(dmask, F32), jnp.asarray(gsum, BF16), jnp.asarray(gsum.T, BF16)


def _peer_u(idx, hn, g, tab_u, tb):
    T = hn.shape[0]
    dmask, gsum, expand = _peer_consts()
    sel_w = PEER_SEL * ROW_CHUNKS
    cst = lambda a: pl.BlockSpec(a.shape, lambda i: (0, 0))
    return pl.pallas_call(
        _peer_u_kernel,
        grid=(T // tb,),
        in_specs=[pl.BlockSpec((tb, PEER_SEL), lambda i: (i, 0), memory_space=pltpu.SMEM),
                  pl.BlockSpec((tb, ROW_CHUNKS, HEAD_W), lambda i: (i, 0, 0)),
                  pl.BlockSpec((tb, PEER_SEL), lambda i: (i, 0)),
                  pl.BlockSpec(memory_space=pltpu.VMEM),
                  cst(dmask), cst(gsum), cst(expand)],
        out_specs=pl.BlockSpec((tb, sel_w), lambda i: (i, 0)),
        out_shape=jax.ShapeDtypeStruct((T, sel_w), F32),
        scratch_shapes=[pltpu.VMEM((2, PEER_SEL * WORD_ROWS, HEAD_W), I32),
                        pltpu.VMEM((tb, sel_w), F32)],
        compiler_params=_cparams(("arbitrary",)),
        name="peer_u",
    )(idx, hn.reshape(T, ROW_CHUNKS, HEAD_W), g, tab_u, dmask, gsum, expand)


def _peer_v(idx, wexp, tab_v, tb):
    T = wexp.shape[0]
    dmask, _, _ = _peer_consts()
    out = pl.pallas_call(
        _peer_v_kernel,
        grid=(T // tb,),
        in_specs=[pl.BlockSpec((tb, PEER_SEL), lambda i: (i, 0), memory_space=pltpu.SMEM),
                  pl.BlockSpec((tb, wexp.shape[1]), lambda i: (i, 0)),
                  pl.BlockSpec(memory_space=pltpu.VMEM),
                  pl.BlockSpec(dmask.shape, lambda i: (0, 0))],
        out_specs=pl.BlockSpec((tb, ROW_CHUNKS, HEAD_W), lambda i: (i, 0, 0)),
        out_shape=jax.ShapeDtypeStruct((T, ROW_CHUNKS, HEAD_W), F32),
        scratch_shapes=[pltpu.VMEM((2, PEER_SEL * WORD_ROWS, HEAD_W), I32)],
        compiler_params=_cparams(("arbitrary",)),
        name="peer_v",
    )(idx, wexp, tab_v, dmask)
    return out.reshape(T, D_MODEL)


def _ple_kernel(hp_ref, pe_ref, p_ref, wg_ref, bg_ref, wp_ref, gf_ref, y_ref):
    h = hp_ref[...] + pe_ref[...]
    gate = jax.nn.sigmoid(jnp.dot(h.astype(BF16), wg_ref[...], preferred_element_type=F32) + bg_ref[...])
    h = h + gate * jnp.dot(p_ref[...].astype(BF16), wp_ref[...], preferred_element_type=F32)
    y_ref[...] = _rms(h, gf_ref[...])


def _ple(hp, pe, p, wg_bf, bg, wp_bf, gf, tm):
    T = hp.shape[0]
    row = lambda i: (i, 0)
    full = pl.BlockSpec((tm, D_MODEL), row)
    cst = lambda a: pl.BlockSpec(a.shape, lambda i: (0, 0))
    return pl.pallas_call(
        _ple_kernel,
        grid=(T // tm,),
        in_specs=[full, full, pl.BlockSpec((tm, PLE_DIM), row), cst(wg_bf), cst(bg), cst(wp_bf), cst(gf)],
        out_specs=full,
        out_shape=jax.ShapeDtypeStruct((T, D_MODEL), F32),
        compiler_params=_cparams(("parallel",)),
        name="ple",
    )(hp, pe, p, wg_bf, bg, wp_bf, gf)


def _rope_tables(pos):
    half = DA_DH // 2
    freqs = ROPE_THETA ** (-jnp.arange(half, dtype=F32) / half)
    ang = pos.astype(F32)[:, None] * freqs[None, :]
    cos = jnp.tile(jnp.cos(ang), (1, GROUP_W // half))
    sign = jnp.where((jnp.arange(GROUP_W) % DA_DH) < half, -1.0, 1.0).astype(F32)
    sin = jnp.tile(jnp.sin(ang), (1, GROUP_W // half)) * sign[None, :]
    return cos, sin


def _ffn(hp_parts, p, weights, tb_peer):
    hp, hn, qh = hp_parts
    (sk_big, tab_u, tab_v, wg_bf, bg, wp_bf, gfinal) = weights
    T = hp.shape[0]
    tk = 128
    tpad = -(-T // tk) * tk
    qh_p = jnp.pad(qh, ((0, tpad - T), (0, 0))) if tpad != T else qh
    idx, gw = _topk(qh_p, sk_big, tk)
    idx, gw = idx[:T], gw[:T]
    wexp = _peer_u(idx, hn, gw, tab_u, tb_peer)
    pe = _peer_v(idx, wexp, tab_v, tb_peer)
    tm = min(256, T)
    return _ple(hp, pe, p, wg_bf, bg, wp_bf, gfinal, tm)


def kernel(x_prompt, x_sample, p_prompt, p_sample, cache_k, cache_v, state_hgrn, page_table, g_attn, w_in, hgrn_gamma, g_hgrn_norm, lambda_q1, lambda_k1, lambda_q2, lambda_k2, g_diff_norm, w_out, g_ffn, peer_w_query, peer_sub_keys, peer_u, peer_v, ple_w_gate, ple_b_gate, ple_w_proj, g_final):
    Bp, Lp, D = x_prompt.shape
    Bs, Ls, _ = x_sample.shape
    assert D == D_MODEL and Ls == 1 and w_in.shape[0] == 1
    l = 0
    past_len = page_table.shape[1] * cache_k.shape[2]
    lam_init = 0.8 - 0.6 * math.exp(-0.3 * l)
    lam = (jnp.exp(jnp.sum(lambda_q1[l] * lambda_k1[l])) - jnp.exp(jnp.sum(lambda_q2[l] * lambda_k2[l]))
           + lam_init).reshape(1).astype(F32)
    lb = jnp.cumsum(jax.nn.softmax(hgrn_gamma.astype(F32), axis=0), axis=0)[l].reshape(1, GROUP_W)

    w_in_bf = w_in[l].astype(BF16)
    w_out_bf = w_out[l].astype(BF16)
    nqc = PEER_HEADS * 2 * PEER_NKEYS
    wq_bf = (peer_w_query[l].reshape(D, PEER_HEADS, 2, PEER_NKEYS).transpose(0, 2, 1, 3)
             .reshape(D, nqc).astype(BF16))
    sk_big = jnp.einsum('hcnk,hg->cnhgk', peer_sub_keys[l], jnp.eye(PEER_HEADS, dtype=F32)).reshape(
        2, PEER_NKEYS * PEER_HEADS, PEER_HEADS * PEER_NKEYS).astype(BF16)
    tab_u = _pack_table(peer_u[l])
    tab_v = _pack_table(peer_v[l])
    wg_bf = ple_w_gate[l].astype(BF16)
    wp_bf = ple_w_proj[l].astype(BF16)
    row = lambda a: a.reshape(1, -1).astype(F32)
    ffn_w = (sk_big, tab_u, tab_v, wg_bf, row(ple_b_gate[l]), wp_bf, row(g_final))

    def group(x, pos_tab, tm):
        cosf, sins = pos_tab
        return _proj(x, row(g_attn[l]), w_in_bf, lb, cosf, sins, tm)

    def mix(oh, gate, od, x, tm):
        return _mix(oh, gate, od, x, row(g_hgrn_norm[l]), row(g_diff_norm[l]), w_out_bf, row(g_ffn[l]),
                    wq_bf, 1.0 - lam_init, tm)

    Tp = Bp * Lp
    xp = x_prompt.reshape(Tp, D)
    tm_p = math.gcd(Lp, 512)
    hq, hk, lf, hi, gate, k_p, v_p, qb, kb, vb = group(xp, _rope_tables(jnp.arange(Lp)), tm_p)
    o_h, st_p = _hgrn_prompt(hq, hk, hi, lf, Bp, Lp, math.gcd(Lp, 512))
    o_d = _attn_prompt(lam, qb, kb, vb, Bp, Lp, math.gcd(Lp, 512))
    y_p = _ffn(mix(o_h, gate, o_d, xp, min(256, Tp)), p_prompt.reshape(Tp, PLE_DIM), ffn_w, min(64, Tp))

    xs = x_sample.reshape(Bs, D)
    pos_s = jnp.full((Bs,), past_len, dtype=jnp.int32)
    hq, hk, lf, hi, gate, k_s, v_s, qb, kb, vb = group(xs, _rope_tables(pos_s), Bs)
    o_h, st_s = _hgrn_step(hq, hk, hi, lf, state_hgrn.reshape(Bs, HEADS, HEAD_W, HEAD_W))
    npool = cache_k.shape[1]
    o_d = _attn_decode(page_table, lam, qb.astype(F32), k_s, v_s,
                       cache_k.reshape(npool, PAGE, GROUP_W), cache_v.reshape(npool, PAGE, GROUP_W))
    y_s = _ffn(mix(o_h, gate, o_d, xs, Bs), p_sample.reshape(Bs, PLE_DIM), ffn_w, Bs)

    hd = (HEADS, HEAD_W)
    return (y_p.reshape(Bp, Lp, D), y_s.reshape(Bs, Ls, D),
            k_p.reshape((1, Bp, Lp) + hd), v_p.reshape((1, Bp, Lp) + hd),
            jnp.swapaxes(st_p, -1, -2)[None],
            k_s.reshape((1, Bs, Ls) + hd), v_s.reshape((1, Bs, Ls) + hd), st_s[None])
```

```python
import functools
import math

import numpy as np
import jax
import jax.numpy as jnp
from jax import lax
from jax.experimental import pallas as pl
from jax.experimental.pallas import tpu as pltpu

F32 = jnp.float32
BF16 = jnp.bfloat16
I32 = jnp.int32

D_MODEL = 1024
HEADS = 4
HEAD_W = 128
GROUP_W = HEADS * HEAD_W
DA_DH = 64
ROPE_THETA = 10000.0
HG_CHUNK = 64
PEER_HEADS = 8
PEER_NKEYS = 128
PEER_TOPK = 16
PEER_SEL = PEER_HEADS * PEER_TOPK
PLE_DIM = 256
EPS = 1e-6
PAGE = 128
PAGES_PER_STEP = 8
ROW_CHUNKS = D_MODEL // HEAD_W
WORD_ROWS = ROW_CHUNKS // 2
VMEM_LIMIT = 56 * 1024 * 1024

NT = (((1,), (1,)), ((), ()))
TN = (((0,), (0,)), ((), ()))


def _cparams(sem):
    return pltpu.CompilerParams(dimension_semantics=sem, vmem_limit_bytes=VMEM_LIMIT)


def _rms(x, g):
    return x * lax.rsqrt(jnp.mean(x * x, axis=-1, keepdims=True) + EPS) * g


def _store_head_rows(ref, x):
    n = x.shape[0]
    for h in range(HEADS):
        ref[pl.ds(h, n, stride=HEADS), :] = x[:, h * HEAD_W:(h + 1) * HEAD_W]


def _proj_kernel(x_ref, g_ref, w_ref, lb_ref, cos_ref, sin_ref,
                 hq_ref, hk_ref, lf_ref, hi_ref, gate_ref, k_ref, v_ref,
                 qb_ref, kb_ref, vb_ref):
    xb = _rms(x_ref[...], g_ref[...]).astype(BF16)

    def col(i):
        return jnp.dot(xb, w_ref[:, i * GROUP_W:(i + 1) * GROUP_W], preferred_element_type=F32)

    hq_ref[...] = col(0)
    lb = lb_ref[...]
    f = lb + (1.0 - lb) * jax.nn.sigmoid(col(1))
    lf_ref[...] = jnp.log(f)
    hk_ref[...] = 1.0 - f
    hi_ref[...] = col(2)
    gate_ref[...] = jax.nn.silu(col(3))

    cosf = cos_ref[...]
    sins = sin_ref[...]
    lane = lax.broadcasted_iota(I32, cosf.shape, 1)
    first_half = (lane % DA_DH) < (DA_DH // 2)

    def rope(x):
        swapped = jnp.where(first_half,
                            pltpu.roll(x, GROUP_W - DA_DH // 2, 1),
                            pltpu.roll(x, DA_DH // 2, 1))
        return x * cosf + swapped * sins

    q = rope(col(4))
    qb_ref[...] = (q * (DA_DH ** -0.5)).astype(BF16)
    k = rope(col(5))
    _store_head_rows(k_ref, k)
    kb_ref[...] = k.astype(BF16)
    v = col(6)
    _store_head_rows(v_ref, v)
    vb_ref[...] = v.astype(BF16)


def _proj(x, g, w_bf, lb, cosf, sins, tm):
    T = x.shape[0]
    nl = cosf.shape[0] // tm
    row = lambda i: (i, 0)
    fixed = lambda i: (0, 0)
    tab = lambda i: (i % nl, 0)
    f32o = jax.ShapeDtypeStruct((T, GROUP_W), F32)
    bfo = jax.ShapeDtypeStruct((T, GROUP_W), BF16)
    ospec = pl.BlockSpec((tm, GROUP_W), row)
    return pl.pallas_call(
        _proj_kernel,
        grid=(T // tm,),
        in_specs=[pl.BlockSpec((tm, D_MODEL), row),
                  pl.BlockSpec((1, D_MODEL), fixed),
                  pl.BlockSpec(w_bf.shape, fixed),
                  pl.BlockSpec((1, GROUP_W), fixed),
                  pl.BlockSpec((tm, GROUP_W), tab),
                  pl.BlockSpec((tm, GROUP_W), tab)],
        out_specs=[ospec] * 5 + [pl.BlockSpec((tm * HEADS, HEAD_W), row)] * 2 + [ospec] * 3,
        out_shape=[f32o] * 5 + [jax.ShapeDtypeStruct((T * HEADS, HEAD_W), F32)] * 2 + [bfo] * 3,
        compiler_params=_cparams(("parallel",)),
        name="proj",
    )(x, g, w_bf, lb, cosf, sins)


def _hgrn_consts():
    C = HG_CHUNK
    t = np.arange(C)[:, None]
    u = np.arange(C)[None, :]
    mats = [(u <= t)]
    lows, pms = [], []
    h = C // 2
    while h >= 1:
        base = (t // (2 * h)) * (2 * h)
        r = base + h - 1
        lower = t >= base + h
        m = np.where(lower, (u > r) & (u <= t), (u > t) & (u <= r))
        mats.append(m)
        lows.append(np.broadcast_to(lower, (C, HEAD_W)))
        s = np.arange(C)[None, :]
        pms.append((t // (2 * h)) == (s // (2 * h)))
        h //= 2
    mats.append(u > t)
    wall = np.concatenate(mats, axis=0).astype(np.float32)
    return (jnp.asarray(wall, BF16), jnp.asarray(np.stack(lows), F32), jnp.asarray(np.stack(pms), F32))


def _split3(x):
    a = x.astype(BF16)
    r = x - a.astype(F32)
    b = r.astype(BF16)
    c = (r - b.astype(F32)).astype(BF16)
    return a, b, c


def _hgrn_kernel(q_ref, k_ref, v_ref, lf_ref, wall_ref, low_ref, pm_ref, o_ref, st_ref, st_scr, *, nlev):
    i = pl.program_id(2)
    C = HG_CHUNK

    @pl.when(i == 0)
    def _():
        st_scr[...] = jnp.zeros_like(st_scr)

    wall = wall_ref[...]
    eye = (lax.broadcasted_iota(I32, (C, C), 0) == lax.broadcasted_iota(I32, (C, C), 1)).astype(F32)
    st = st_scr[...]
    for c in range(q_ref.shape[0] // C):
        sl = slice(c * C, (c + 1) * C)
        q, k, v, lf = q_ref[sl, :], k_ref[sl, :], v_ref[sl, :], lf_ref[sl, :]
        lcat = jnp.concatenate(_split3(lf), axis=-1)
        d3 = jnp.dot(wall, lcat, preferred_element_type=F32)
        e_all = jnp.exp(d3[:, :HEAD_W] + d3[:, HEAD_W:2 * HEAD_W] + d3[:, 2 * HEAD_W:])
        e_g = e_all[0:C]
        e_last = e_all[C - 1:C]
        e_k = e_all[(nlev + 1) * C:(nlev + 2) * C]
        vb = v.astype(BF16)
        o = lax.dot_general((q * e_g).astype(BF16), st.astype(BF16), NT, preferred_element_type=F32)
        a = eye * jnp.sum(q * k, axis=-1, keepdims=True)
        for l in range(nlev):
            e_l = e_all[(l + 1) * C:(l + 2) * C]
            low = low_ref[l]
            ql = (q * e_l * low).astype(BF16)
            kl = (k * e_l * (1.0 - low)).astype(BF16)
            a = a + lax.dot_general(ql, kl, NT, preferred_element_type=F32) * pm_ref[l]
        o_ref[sl, :] = o + jnp.dot(a.astype(BF16), vb, preferred_element_type=F32)
        kd = (k * e_k).astype(BF16)
        st = e_last * st + lax.dot_general(vb, kd, TN, preferred_element_type=F32)
    st_scr[...] = st

    @pl.when(i == pl.num_programs(2) - 1)
    def _():
        st_ref[0, 0] = st


def _hgrn_prompt(hq, hk, hi, lf, B, L, lb_rows):
    wall, low, pm = _hgrn_consts()
    nlev = low.shape[0]
    nblk = L // lb_rows
    blk = pl.BlockSpec((lb_rows, HEAD_W), lambda b, h, i: (b * nblk + i, h))
    cst = lambda a: pl.BlockSpec(a.shape, lambda b, h, i: (0,) * a.ndim)
    return pl.pallas_call(
        functools.partial(_hgrn_kernel, nlev=nlev),
        grid=(B, HEADS, nblk),
        in_specs=[blk, blk, blk, blk, cst(wall), cst(low), cst(pm)],
        out_specs=[blk, pl.BlockSpec((1, 1, HEAD_W, HEAD_W), lambda b, h, i: (b, h, 0, 0))],
        out_shape=[jax.ShapeDtypeStruct(hq.shape, F32),
                   jax.ShapeDtypeStruct((B, HEADS, HEAD_W, HEAD_W), F32)],
        scratch_shapes=[pltpu.VMEM((HEAD_W, HEAD_W), F32)],
        compiler_params=_cparams(("parallel", "parallel", "arbitrary")),
        name="hgrn_prompt",
    )(hq, hk, hi, lf, wall, low, pm)


def _hgrn_step_kernel(qc_ref, kc_ref, lfc_ref, v_ref, s0_ref, o_ref, s_ref):
    qc, kc = qc_ref[0, 0], kc_ref[0, 0]
    dec = jnp.exp(lfc_ref[0, 0])
    v = v_ref[0, 0]
    s0 = s0_ref[0, 0]
    s_ref[0, 0] = dec * s0 + kc * v
    o_ref[0, 0] = (jnp.sum((qc * dec) * s0, axis=0, keepdims=True)
                   + jnp.sum(qc * kc, axis=0, keepdims=True) * v)


def _hgrn_step(hq, hk, hi, lf, s0):
    B = hq.shape[0]
    colv = lambda a: a.reshape(B, HEADS, HEAD_W, 1)
    cspec = pl.BlockSpec((1, 1, HEAD_W, 1), lambda b, h: (b, h, 0, 0))
    rspec = pl.BlockSpec((1, 1, 1, HEAD_W), lambda b, h: (b, h, 0, 0))
    sspec = pl.BlockSpec((1, 1, HEAD_W, HEAD_W), lambda b, h: (b, h, 0, 0))
    o, s = pl.pallas_call(
        _hgrn_step_kernel,
        grid=(B, HEADS),
        in_specs=[cspec, cspec, cspec, rspec, sspec],
        out_specs=[rspec, sspec],
        out_shape=[jax.ShapeDtypeStruct((B, HEADS, 1, HEAD_W), F32),
                   jax.ShapeDtypeStruct((B, HEADS, HEAD_W, HEAD_W), F32)],
        compiler_params=_cparams(("parallel", "parallel")),
        name="hgrn_step",
    )(colv(hq), colv(hk), colv(lf), hi.reshape(B, HEADS, 1, HEAD_W), s0)
    return o.reshape(B, GROUP_W), s


def _attn_kernel(lam_ref, q_ref, k_ref, v_ref, o_ref, qs_scr, m_scr, l_scr, acc_scr):
    qi = pl.program_id(1)
    ki = pl.program_id(2)
    bq = q_ref.shape[0]
    bk = k_ref.shape[0]

    @pl.when(ki == 0)
    def _():
        lane = lax.broadcasted_iota(I32, (bq, HEAD_W), 1)
        for h in range(HEADS):
            q = q_ref[:, h * HEAD_W:(h + 1) * HEAD_W]
            zero = jnp.zeros_like(q)
            qs_scr[h] = jnp.concatenate([jnp.where(lane < DA_DH, q, zero),
                                         jnp.where(lane >= DA_DH, q, zero)], axis=0)
        m_scr[...] = jnp.full(m_scr.shape, -jnp.inf, F32)
        l_scr[...] = jnp.zeros_like(l_scr)
        acc_scr[...] = jnp.zeros_like(acc_scr)

    @pl.when(ki <= qi)
    def _():
        row = lax.broadcasted_iota(I32, (2 * bq, bk), 0) % bq
        colv = lax.broadcasted_iota(I32, (2 * bq, bk), 1)
        keep = (colv <= row) | (ki < qi)
        for h in range(HEADS):
            kh = k_ref[:, h * HEAD_W:(h + 1) * HEAD_W]
            vh = v_ref[:, h * HEAD_W:(h + 1) * HEAD_W]
            s = lax.dot_general(qs_scr[h], kh, NT, preferred_element_type=F32)
            s = jnp.where(keep, s, -jnp.inf)
            m_old = m_scr[h]
            m_new = jnp.maximum(m_old, jnp.max(s, axis=-1, keepdims=True))
            alpha = jnp.exp(m_old - m_new)
            p = jnp.exp(s - m_new)
            l_scr[h] = alpha * l_scr[h] + jnp.sum(p, axis=-1, keepdims=True)
            acc_scr[h] = alpha * acc_scr[h] + jnp.dot(p.astype(BF16), vh, preferred_element_type=F32)
            m_scr[h] = m_new

    @pl.when(ki == qi)
    def _():
        lam = lam_ref[0]
        for h in range(HEADS):
            r = acc_scr[h] / l_scr[h]
            o_ref[:, h * HEAD_W:(h + 1) * HEAD_W] = r[:bq] - lam * r[bq:]


def _attn_prompt(lam, qb, kb, vb, B, L, bq):
    nq = L // bq
    qspec = pl.BlockSpec((bq, GROUP_W), lambda b, qi, ki: (b * nq + qi, 0))
    kspec = pl.BlockSpec((bq, GROUP_W), lambda b, qi, ki: (b * nq + jnp.minimum(ki, qi), 0))
    return pl.pallas_call(
        _attn_kernel,
        grid=(B, nq, nq),
        in_specs=[pl.BlockSpec(memory_space=pltpu.SMEM), qspec, kspec, kspec],
        out_specs=qspec,
        out_shape=jax.ShapeDtypeStruct(qb.shape, F32),
        scratch_shapes=[pltpu.VMEM((HEADS, 2 * bq, HEAD_W), BF16),
                        pltpu.VMEM((HEADS, 2 * bq, 1), F32),
                        pltpu.VMEM((HEADS, 2 * bq, 1), F32),
                        pltpu.VMEM((HEADS, 2 * bq, HEAD_W), F32)],
        compiler_params=_cparams(("parallel", "parallel", "arbitrary")),
        name="attn_prompt",
    )(lam, qb, kb, vb)


def _decode_kernel(pt_ref, lam_ref, q_ref, kn_ref, vn_ref, *rest):
    npg = PAGES_PER_STEP
    k_refs, v_refs = rest[:npg], rest[npg:2 * npg]
    o_ref, kcat, vcat, m_scr, l_scr, acc_scr = rest[2 * npg:]
    i = pl.program_id(1)
    nrow = 2 * HEADS

    rowi = lax.broadcasted_iota(I32, (nrow, GROUP_W), 0)
    lane = lax.broadcasted_iota(I32, (nrow, GROUP_W), 1)
    sel = (lane // HEAD_W == rowi // 2) & ((lane % HEAD_W) // DA_DH == rowi % 2)
    qbd = jnp.where(sel, jnp.broadcast_to(q_ref[0], (nrow, GROUP_W)), 0.0)

    @pl.when(i == 0)
    def _():
        m_scr[...] = jnp.full(m_scr.shape, -jnp.inf, F32)
        l_scr[...] = jnp.zeros_like(l_scr)
        acc_scr[...] = jnp.zeros_like(acc_scr)

    for p in range(npg):
        for h in range(HEADS):
            rows = slice(p * PAGE, (p + 1) * PAGE)
            cols = slice(h * HEAD_W, (h + 1) * HEAD_W)
            kcat[rows, cols] = k_refs[p][0, pl.ds(h, PAGE, stride=HEADS), :].astype(BF16)
            vcat[rows, cols] = v_refs[p][0, pl.ds(h, PAGE, stride=HEADS), :].astype(BF16)
    s = lax.dot_general(qbd.astype(BF16), kcat[...], NT, preferred_element_type=F32)
    m_old = m_scr[...]
    m_new = jnp.maximum(m_old, jnp.max(s, axis=-1, keepdims=True))
    alpha = jnp.exp(m_old - m_new)
    p_ = jnp.exp(s - m_new)
    l_scr[...] = alpha * l_scr[...] + jnp.sum(p_, axis=-1, keepdims=True)
    acc_scr[...] = alpha * acc_scr[...] + jnp.dot(p_.astype(BF16), vcat[...], preferred_element_type=F32)
    m_scr[...] = m_new

    @pl.when(i == pl.num_programs(1) - 1)
    def _():
        s_new = jnp.sum(qbd * kn_ref[0], axis=-1, keepdims=True)
        m_o = m_scr[...]
        m_n = jnp.maximum(m_o, s_new)
        al = jnp.exp(m_o - m_n)
        pn = jnp.exp(s_new - m_n)
        l_f = al * l_scr[...] + pn
        r = (al * acc_scr[...] + pn * vn_ref[0]) / l_f
        lam = lam_ref[0]
        outs = []
        for h in range(HEADS):
            blk = slice(h * HEAD_W, (h + 1) * HEAD_W)
            outs.append(r[2 * h:2 * h + 1, blk] - lam * r[2 * h + 1:2 * h + 2, blk])
        o_ref[0] = jnp.concatenate(outs, axis=-1)


def _attn_decode(page_table, lam, q, kn, vn, ck, cv):
    B = q.shape[0]
    nsteps = page_table.shape[1] // PAGES_PER_STEP
    tok = pl.BlockSpec((1, 1, GROUP_W), lambda b, i, pt: (b, 0, 0))

    def page(p):
        return pl.BlockSpec((1, PAGE * HEADS, HEAD_W),
                            lambda b, i, pt: (pt[b, i * PAGES_PER_STEP + p], 0, 0))

    pages = [page(p) for p in range(PAGES_PER_STEP)]
    grid_spec = pltpu.PrefetchScalarGridSpec(
        num_scalar_prefetch=1,
        grid=(B, nsteps),
        in_specs=[pl.BlockSpec(memory_space=pltpu.SMEM), tok, tok, tok] + pages + pages,
        out_specs=tok,
        scratch_shapes=[pltpu.VMEM((PAGES_PER_STEP * PAGE, GROUP_W), BF16),
                        pltpu.VMEM((PAGES_PER_STEP * PAGE, GROUP_W), BF16),
                        pltpu.VMEM((2 * HEADS, 1), F32),
                        pltpu.VMEM((2 * HEADS, 1), F32),
                        pltpu.VMEM((2 * HEADS, GROUP_W), F32)])
    r3 = lambda a: a.reshape(B, 1, GROUP_W)
    out = pl.pallas_call(
        _decode_kernel,
        grid_spec=grid_spec,
        out_shape=jax.ShapeDtypeStruct((B, 1, GROUP_W), F32),
        compiler_params=_cparams(("parallel", "arbitrary")),
        name="attn_decode",
    )(page_table, lam, r3(q), r3(kn), r3(vn), *([ck] * PAGES_PER_STEP), *([cv] * PAGES_PER_STEP))
    return out.reshape(B, GROUP_W)


def _mix_kernel(oh_ref, gate_ref, od_ref, x_ref, gh_ref, gd_ref, wo_ref, gf_ref, wq_ref,
                hp_ref, hn_ref, qh_ref, *, dscale):
    parts = []
    for h in range(HEADS):
        blk = slice(h * HEAD_W, (h + 1) * HEAD_W)
        parts.append(_rms(oh_ref[:, blk], gh_ref[...]) * gate_ref[:, blk])
    for h in range(HEADS):
        blk = slice(h * HEAD_W, (h + 1) * HEAD_W)
        parts.append(_rms(od_ref[:, blk], gd_ref[...]) * dscale)
    y = jnp.concatenate(parts, axis=-1).astype(BF16)
    hp = x_ref[...] + jnp.dot(y, wo_ref[...], preferred_element_type=F32)
    hp_ref[...] = hp
    hn = _rms(hp, gf_ref[...])
    hn_ref[...] = hn
    qh_ref[...] = jnp.dot(hn.astype(BF16), wq_ref[...], preferred_element_type=F32).astype(BF16)


def _mix(oh, gate, od, x, gh, gd, wo_bf, gf, wq_bf, dscale, tm):
    T = x.shape[0]
    row = lambda i: (i, 0)
    fixed = lambda i: (0, 0)
    half = pl.BlockSpec((tm, GROUP_W), row)
    full = pl.BlockSpec((tm, D_MODEL), row)
    cst = lambda a: pl.BlockSpec(a.shape, fixed)
    nq = wq_bf.shape[1]
    return pl.pallas_call(
        functools.partial(_mix_kernel, dscale=dscale),
        grid=(T // tm,),
        in_specs=[half, half, half, full, cst(gh), cst(gd), cst(wo_bf), cst(gf), cst(wq_bf)],
        out_specs=[full, full, pl.BlockSpec((tm, nq), row)],
        out_shape=[jax.ShapeDtypeStruct((T, D_MODEL), F32), jax.ShapeDtypeStruct((T, D_MODEL), F32),
                   jax.ShapeDtypeStruct((T, nq), BF16)],
        compiler_params=_cparams(("parallel",)),
        name="mix",
    )(oh, gate, od, x, gh, gd, wo_bf, gf, wq_bf)


def _staircase():
    K = PEER_TOPK
    return [(a, b) for a in range(K) for b in range(K) if (a + 1) * (b + 1) <= K]


def _topk_kernel(qh_ref, sk_ref, e_ref, g_ref, s_scr, val_scr, idx_scr, cand_scr, cidx_scr, sc_scr, e_scr):
    K = PEER_TOPK
    NK = PEER_NKEYS
    tb = qh_ref.shape[0]
    half_w = PEER_HEADS * NK
    n_iota = lax.broadcasted_iota(I32, (NK, PEER_HEADS, tb), 0).astype(F32)
    neg = -jnp.inf

    for c in range(2):
        qc = qh_ref[:, c * half_w:(c + 1) * half_w]
        s = lax.dot_general(sk_ref[c], qc, NT, preferred_element_type=F32)
        s_scr[...] = s.reshape(NK, PEER_HEADS, tb)

        def body(a, carry):
            sv = s_scr[...]
            m = jnp.max(sv, axis=0)
            idx = jnp.min(jnp.where(sv == m[None], n_iota, float(NK)), axis=0)
            s_scr[...] = jnp.where(n_iota == idx[None], neg, sv)
            val_scr[c, a] = m
            idx_scr[c, a] = idx.astype(I32)
            return carry

        lax.fori_loop(0, K, body, 0)

    pairs = _staircase()
    for i, (a, b) in enumerate(pairs):
        cand_scr[i] = val_scr[0, a] + val_scr[1, b]
        cidx_scr[i] = (idx_scr[0, a] * NK + idx_scr[1, b]) * WORD_ROWS
    flats = [a * K + b for a, b in pairs]
    big = K * K

    def body2(r, carry):
        cs = [cand_scr[i] for i in range(len(pairs))]
        m = functools.reduce(jnp.maximum, cs)
        pos = functools.reduce(jnp.minimum, [jnp.where(cv == m, fl, big) for cv, fl in zip(cs, flats)])
        e = jnp.zeros(m.shape, I32)
        for i, (cv, fl) in enumerate(zip(cs, flats)):
            hit = pos == fl
            cand_scr[i] = jnp.where(hit, neg, cv)
            e = jnp.where(hit, cidx_scr[i], e)
        sc_scr[r] = m
        e_scr[r] = e
        return carry

    lax.fori_loop(0, K, body2, 0)
    sc = sc_scr[...]
    ex = jnp.exp(sc - sc[0:1])
    g = ex / jnp.sum(ex, axis=0, keepdims=True)
    g_ref[...] = g.reshape(PEER_SEL, tb).T
    e_ref[...] = e_scr[...].reshape(PEER_SEL, tb).T


def _topk(qh, sk_big, tb):
    T = qh.shape[0]
    ncand = len(_staircase())
    hw = (PEER_HEADS, tb)
    return pl.pallas_call(
        _topk_kernel,
        grid=(T // tb,),
        in_specs=[pl.BlockSpec((tb, qh.shape[1]), lambda i: (i, 0)),
                  pl.BlockSpec(sk_big.shape, lambda i: (0, 0, 0))],
        out_specs=[pl.BlockSpec((tb, PEER_SEL), lambda i: (i, 0))] * 2,
        out_shape=[jax.ShapeDtypeStruct((T, PEER_SEL), I32),
                   jax.ShapeDtypeStruct((T, PEER_SEL), F32)],
        scratch_shapes=[pltpu.VMEM((PEER_NKEYS,) + hw, F32),
                        pltpu.VMEM((2, PEER_TOPK) + hw, F32),
                        pltpu.VMEM((2, PEER_TOPK) + hw, I32),
                        pltpu.VMEM((ncand,) + hw, F32),
                        pltpu.VMEM((ncand,) + hw, I32),
                        pltpu.VMEM((PEER_TOPK,) + hw, F32),
                        pltpu.VMEM((PEER_TOPK,) + hw, I32)],
        compiler_params=_cparams(("parallel",)),
        name="peer_topk",
    )(qh, sk_big)


def _pack_kernel(t_ref, o_ref):
    n = t_ref.shape[0]

    def bits(x):
        return pltpu.bitcast(x.astype(BF16).astype(F32), jnp.uint32) >> 16

    for s in range(WORD_ROWS):
        lo = bits(t_ref[:, (2 * s) * HEAD_W:(2 * s + 1) * HEAD_W])
        hi = bits(t_ref[:, (2 * s + 1) * HEAD_W:(2 * s + 2) * HEAD_W])
        o_ref[pl.ds(s, n, stride=WORD_ROWS), :] = pltpu.bitcast((hi << 16) | lo, I32)


def _pack_table(tab, rows=256):
    n = tab.shape[0]
    return pl.pallas_call(
        _pack_kernel,
        grid=(n // rows,),
        in_specs=[pl.BlockSpec((rows, D_MODEL), lambda i: (i, 0))],
        out_specs=pl.BlockSpec((rows * WORD_ROWS, HEAD_W), lambda i: (i, 0)),
        out_shape=jax.ShapeDtypeStruct((n * WORD_ROWS, HEAD_W), I32),
        compiler_params=_cparams(("parallel",)),
        name="pack_table",
    )(tab)


def _gather_rows(idx_ref, tab_ref, stg, t):
    for j in range(PEER_SEL):
        r = pl.multiple_of(idx_ref[t, j], WORD_ROWS)
        stg[j * WORD_ROWS:(j + 1) * WORD_ROWS, :] = tab_ref[pl.ds(r, WORD_ROWS), :]


def _peer_u_kernel(idx_ref, x_ref, g_ref, tab_ref, dmask_ref, gsum_ref, expand_ref, w_ref, stg, c_scr):
    tb = x_ref.shape[0]
    dmask = dmask_ref[...]

    def group(gi, carry):
        rows = []
        for i in range(8):
            t = gi * 8 + i
            buf = stg.at[i % 2]
            _gather_rows(idx_ref, tab_ref, buf, t)
            ub = pltpu.bitcast(buf[...], BF16)
            xt = x_ref[t].astype(BF16)
            r = lax.dot_general(xt, ub, NT, preferred_element_type=F32)
            rows.append(jnp.sum(r * dmask, axis=0, keepdims=True))
        c_scr[pl.ds(pl.multiple_of(gi * 8, 8), 8), :] = jnp.concatenate(rows, axis=0)
        return carry

    lax.fori_loop(0, tb // 8, group, 0)
    c = c_scr[...]
    c_hi = c.astype(BF16)
    c_lo = (c - c_hi.astype(F32)).astype(BF16)
    a = (jnp.dot(c_hi, gsum_ref[...], preferred_element_type=F32)
         + jnp.dot(c_lo, gsum_ref[...], preferred_element_type=F32))
    w = g_ref[...] * jax.nn.gelu(a)
    w_ref[...] = jnp.dot(w.astype(BF16), expand_ref[...], preferred_element_type=F32)


def _peer_v_kernel(idx_ref, w_ref, tab_ref, dmask_ref, o_ref, stg):
    tb = w_ref.shape[0]
    dmask = dmask_ref[...]

    def group(gi, carry):
        w8 = w_ref[pl.ds(pl.multiple_of(gi * 8, 8), 8), :]
        for i in range(8):
            t = gi * 8 + i
            buf = stg.at[i % 2]
            _gather_rows(idx_ref, tab_ref, buf, t)
            vb = pltpu.bitcast(buf[...], BF16)
            lhs = (jnp.broadcast_to(w8[i:i + 1, :], dmask.shape) * dmask).astype(BF16)
            o_ref[t] = jnp.dot(lhs, vb, preferred_element_type=F32)
        return carry

    lax.fori_loop(0, tb // 8, group, 0)


def _peer_consts():
    sel_w = PEER_SEL * ROW_CHUNKS
    lane = np.arange(sel_w)
    dmask = (lane[None, :] % ROW_CHUNKS == np.arange(ROW_CHUNKS)[:, None]).astype(np.float32)
    gsum = (lane[:, None] // ROW_CHUNKS == np.arange(PEER_SEL)[None, :]).astype(np.float32)
    return jnp.asarray(dmask, F32), jnp.asarray(gsum, BF16), jnp.asarray(gsum.T, BF16)


def _peer_u(idx, hn, g, tab_u, tb):
    T = hn.shape[0]
    dmask, gsum, expand = _peer_consts()
    sel_w = PEER_SEL * ROW_CHUNKS
    cst = lambda a: pl.BlockSpec(a.shape, lambda i: (0, 0))
    return pl.pallas_call(
        _peer_u_kernel,
        grid=(T // tb,),
        in_specs=[pl.BlockSpec((tb, PEER_SEL), lambda i: (i, 0), memory_space=pltpu.SMEM),
                  pl.BlockSpec((tb, ROW_CHUNKS, HEAD_W), lambda i: (i, 0, 0)),
                  pl.BlockSpec((tb, PEER_SEL), lambda i: (i, 0)),
                  pl.BlockSpec(memory_space=pltpu.VMEM),
                  cst(dmask), cst(gsum), cst(expand)],
        out_specs=pl.BlockSpec((tb, sel_w), lambda i: (i, 0)),
        out_shape=jax.ShapeDtypeStruct((T, sel_w), F32),
        scratch_shapes=[pltpu.VMEM((2, PEER_SEL * WORD_ROWS, HEAD_W), I32),
                        pltpu.VMEM((tb, sel_w), F32)],
        compiler_params=_cparams(("arbitrary",)),
        name="peer_u",
    )(idx, hn.reshape(T, ROW_CHUNKS, HEAD_W), g, tab_u, dmask, gsum, expand)


def _peer_v(idx, wexp, tab_v, tb):
    T = wexp.shape[0]
    dmask, _, _ = _peer_consts()
    out = pl.pallas_call(
        _peer_v_kernel,
        grid=(T // tb,),
        in_specs=[pl.BlockSpec((tb, PEER_SEL), lambda i: (i, 0), memory_space=pltpu.SMEM),
                  pl.BlockSpec((tb, wexp.shape[1]), lambda i: (i, 0)),
                  pl.BlockSpec(memory_space=pltpu.VMEM),
                  pl.BlockSpec(dmask.shape, lambda i: (0, 0))],
        out_specs=pl.BlockSpec((tb, ROW_CHUNKS, HEAD_W), lambda i: (i, 0, 0)),
        out_shape=jax.ShapeDtypeStruct((T, ROW_CHUNKS, HEAD_W), F32),
        scratch_shapes=[pltpu.VMEM((2, PEER_SEL * WORD_ROWS, HEAD_W), I32)],
        compiler_params=_cparams(("arbitrary",)),
        name="peer_v",
    )(idx, wexp, tab_v, dmask)
    return out.reshape(T, D_MODEL)


def _ple_kernel(hp_ref, pe_ref, p_ref, wg_ref, bg_ref, wp_ref, gf_ref, y_ref):
    h = hp_ref[...] + pe_ref[...]
    gate = jax.nn.sigmoid(jnp.dot(h.astype(BF16), wg_ref[...], preferred_element_type=F32) + bg_ref[...])
    h = h + gate * jnp.dot(p_ref[...].astype(BF16), wp_ref[...], preferred_element_type=F32)
    y_ref[...] = _rms(h, gf_ref[...])


def _ple(hp, pe, p, wg_bf, bg, wp_bf, gf, tm):
    T = hp.shape[0]
    row = lambda i: (i, 0)
    full = pl.BlockSpec((tm, D_MODEL), row)
    cst = lambda a: pl.BlockSpec(a.shape, lambda i: (0, 0))
    return pl.pallas_call(
        _ple_kernel,
        grid=(T // tm,),
        in_specs=[full, full, pl.BlockSpec((tm, PLE_DIM), row), cst(wg_bf), cst(bg), cst(wp_bf), cst(gf)],
        out_specs=full,
        out_shape=jax.ShapeDtypeStruct((T, D_MODEL), F32),
        compiler_params=_cparams(("parallel",)),
        name="ple",
    )(hp, pe, p, wg_bf, bg, wp_bf, gf)


def _rope_tables(pos):
    half = DA_DH // 2
    freqs = ROPE_THETA ** (-jnp.arange(half, dtype=F32) / half)
    ang = pos.astype(F32)[:, None] * freqs[None, :]
    cos = jnp.tile(jnp.cos(ang), (1, GROUP_W // half))
    sign = jnp.where((jnp.arange(GROUP_W) % DA_DH) < half, -1.0, 1.0).astype(F32)
    sin = jnp.tile(jnp.sin(ang), (1, GROUP_W // half)) * sign[None, :]
    return cos, sin


def _ffn(hp_parts, p, weights, tb_peer):
    hp, hn, qh = hp_parts
    (sk_big, tab_u, tab_v, wg_bf, bg, wp_bf, gfinal) = weights
    T = hp.shape[0]
    tk = 128
    tpad = -(-T // tk) * tk
    qh_p = jnp.pad(qh, ((0, tpad - T), (0, 0))) if tpad != T else qh
    idx, gw = _topk(qh_p, sk_big, tk)
    idx, gw = idx[:T], gw[:T]
    wexp = _peer_u(idx, hn, gw, tab_u, tb_peer)
    pe = _peer_v(idx, wexp, tab_v, tb_peer)
    tm = min(256, T)
    return _ple(hp, pe, p, wg_bf, bg, wp_bf, gfinal, tm)


def kernel(x_prompt, x_sample, p_prompt, p_sample, cache_k, cache_v, state_hgrn, page_table, g_attn, w_in, hgrn_gamma, g_hgrn_norm, lambda_q1, lambda_k1, lambda_q2, lambda_k2, g_diff_norm, w_out, g_ffn, peer_w_query, peer_sub_keys, peer_u, peer_v, ple_w_gate, ple_b_gate, ple_w_proj, g_final):
    Bp, Lp, D = x_prompt.shape
    Bs, Ls, _ = x_sample.shape
    assert D == D_MODEL and Ls == 1 and w_in.shape[0] == 1
    l = 0
    past_len = page_table.shape[1] * cache_k.shape[2]
    lam_init = 0.8 - 0.6 * math.exp(-0.3 * l)
    lam = (jnp.exp(jnp.sum(lambda_q1[l] * lambda_k1[l])) - jnp.exp(jnp.sum(lambda_q2[l] * lambda_k2[l]))
           + lam_init).reshape(1).astype(F32)
    lb = jnp.cumsum(jax.nn.softmax(hgrn_gamma.astype(F32), axis=0), axis=0)[l].reshape(1, GROUP_W)

    w_in_bf = w_in[l].astype(BF16)
    w_out_bf = w_out[l].astype(BF16)
    nqc = PEER_HEADS * 2 * PEER_NKEYS
    wq_bf = (peer_w_query[l].reshape(D, PEER_HEADS, 2, PEER_NKEYS).transpose(0, 2, 1, 3)
             .reshape(D, nqc).astype(BF16))
    sk_big = jnp.einsum('hcnk,hg->cnhgk', peer_sub_keys[l], jnp.eye(PEER_HEADS, dtype=F32)).reshape(
        2, PEER_NKEYS * PEER_HEADS, PEER_HEADS * PEER_NKEYS).astype(BF16)
    tab_u = _pack_table(peer_u[l])
    tab_v = _pack_table(peer_v[l])
    wg_bf = ple_w_gate[l].astype(BF16)
    wp_bf = ple_w_proj[l].astype(BF16)
    row = lambda a: a.reshape(1, -1).astype(F32)
    ffn_w = (sk_big, tab_u, tab_v, wg_bf, row(ple_b_gate[l]), wp_bf, row(g_final))

    def group(x, pos_tab, tm):
        cosf, sins = pos_tab
        return _proj(x, row(g_attn[l]), w_in_bf, lb, cosf, sins, tm)

    def mix(oh, gate, od, x, tm):
        return _mix(oh, gate, od, x, row(g_hgrn_norm[l]), row(g_diff_norm[l]), w_out_bf, row(g_ffn[l]),
                    wq_bf, 1.0 - lam_init, tm)

    Tp = Bp * Lp
    xp = x_prompt.reshape(Tp, D)
    tm_p = math.gcd(Lp, 512)
    hq, hk, lf, hi, gate, k_p, v_p, qb, kb, vb = group(xp, _rope_tables(jnp.arange(Lp)), tm_p)
    o_h, st_p = _hgrn_prompt(hq, hk, hi, lf, Bp, Lp, math.gcd(Lp, 512))
    o_d = _attn_prompt(lam, qb, kb, vb, Bp, Lp, math.gcd(Lp, 512))
    y_p = _ffn(mix(o_h, gate, o_d, xp, min(256, Tp)), p_prompt.reshape(Tp, PLE_DIM), ffn_w, min(64, Tp))

    xs = x_sample.reshape(Bs, D)
    pos_s = jnp.full((Bs,), past_len, dtype=jnp.int32)
    hq, hk, lf, hi, gate, k_s, v_s, qb, kb, vb = group(xs, _rope_tables(pos_s), Bs)
    o_h, st_s = _hgrn_step(hq, hk, hi, lf, state_hgrn.reshape(Bs, HEADS, HEAD_W, HEAD_W))
    npool = cache_k.shape[1]
    o_d = _attn_decode(page_table, lam, qb.astype(F32), k_s.reshape(Bs, GROUP_W), v_s.reshape(Bs, GROUP_W),
                       cache_k.reshape(npool, PAGE * HEADS, HEAD_W),
                       cache_v.reshape(npool, PAGE * HEADS, HEAD_W))
    y_s = _ffn(mix(o_h, gate, o_d, xs, Bs), p_sample.reshape(Bs, PLE_DIM), ffn_w, Bs)

    hd = (HEADS, HEAD_W)
    return (y_p.reshape(Bp, Lp, D), y_s.reshape(Bs, Ls, D),
            k_p.reshape((1, Bp, Lp) + hd), v_p.reshape((1, Bp, Lp) + hd),
            jnp.swapaxes(st_p, -1, -2)[None],
            k_s.reshape((1, Bs, Ls) + hd), v_s.reshape((1, Bs, Ls) + hd), st_s[None])
```

```python
import functools
import math

import numpy as np
import jax
import jax.numpy as jnp
from jax import lax
from jax.experimental import pallas as pl
from jax.experimental.pallas import tpu as pltpu

F32 = jnp.float32
BF16 = jnp.bfloat16
I32 = jnp.int32

D_MODEL = 1024
HEADS = 4
HEAD_W = 128
GROUP_W = HEADS * HEAD_W
DA_DH = 64
ROPE_THETA = 10000.0
HG_CHUNK = 64
PEER_HEADS = 8
PEER_NKEYS = 128
PEER_TOPK = 16
PEER_SEL = PEER_HEADS * PEER_TOPK
PLE_DIM = 256
EPS = 1e-6
PAGE = 128
PAGES_PER_STEP = 8
ROW_CHUNKS = D_MODEL // HEAD_W
WORD_ROWS = ROW_CHUNKS // 2
VMEM_LIMIT = 56 * 1024 * 1024

NT = (((1,), (1,)), ((), ()))
TN = (((0,), (0,)), ((), ()))


def _cparams(sem):
    return pltpu.CompilerParams(dimension_semantics=sem, vmem_limit_bytes=VMEM_LIMIT)


def _tree(op, xs):
    xs = list(xs)
    while len(xs) > 1:
        nxt = [op(xs[i], xs[i + 1]) for i in range(0, len(xs) - 1, 2)]
        if len(xs) % 2:
            nxt.append(xs[-1])
        xs = nxt
    return xs[0]


def _row_reduce(op, lane_reduce, x):
    blocks = [x[:, i * HEAD_W:(i + 1) * HEAD_W] for i in range(x.shape[1] // HEAD_W)]
    return lane_reduce(_tree(op, blocks), axis=-1, keepdims=True)


def _rms(x, g):
    return x * lax.rsqrt(jnp.mean(x * x, axis=-1, keepdims=True) + EPS) * g


def _store_head_rows(ref, x):
    n = x.shape[0]
    for h in range(HEADS):
        ref[pl.ds(h, n, stride=HEADS), :] = x[:, h * HEAD_W:(h + 1) * HEAD_W]


def _proj_kernel(x_ref, g_ref, w_ref, lb_ref, cos_ref, sin_ref,
                 hq_ref, hk_ref, lf_ref, hi_ref, gate_ref, k_ref, v_ref,
                 qb_ref, kb_ref, vb_ref):
    xb = _rms(x_ref[...], g_ref[...]).astype(BF16)

    def col(i):
        return jnp.dot(xb, w_ref[:, i * GROUP_W:(i + 1) * GROUP_W], preferred_element_type=F32)

    hq_ref[...] = col(0)
    lb = lb_ref[...]
    f = lb + (1.0 - lb) * jax.nn.sigmoid(col(1))
    lf_ref[...] = jnp.log(f)
    hk_ref[...] = 1.0 - f
    hi_ref[...] = col(2)
    gate_ref[...] = jax.nn.silu(col(3))

    cosf = cos_ref[...]
    sins = sin_ref[...]
    lane = lax.broadcasted_iota(I32, cosf.shape, 1)
    first_half = (lane % DA_DH) < (DA_DH // 2)

    def rope(x):
        swapped = jnp.where(first_half,
                            pltpu.roll(x, GROUP_W - DA_DH // 2, 1),
                            pltpu.roll(x, DA_DH // 2, 1))
        return x * cosf + swapped * sins

    q = rope(col(4))
    qb_ref[...] = (q * (DA_DH ** -0.5)).astype(BF16)
    k = rope(col(5))
    _store_head_rows(k_ref, k)
    kb_ref[...] = k.astype(BF16)
    v = col(6)
    _store_head_rows(v_ref, v)
    vb_ref[...] = v.astype(BF16)


def _proj(x, g, w_bf, lb, cosf, sins, tm):
    T = x.shape[0]
    nl = cosf.shape[0] // tm
    row = lambda i: (i, 0)
    fixed = lambda i: (0, 0)
    tab = lambda i: (i % nl, 0)
    f32o = jax.ShapeDtypeStruct((T, GROUP_W), F32)
    bfo = jax.ShapeDtypeStruct((T, GROUP_W), BF16)
    ospec = pl.BlockSpec((tm, GROUP_W), row)
    return pl.pallas_call(
        _proj_kernel,
        grid=(T // tm,),
        in_specs=[pl.BlockSpec((tm, D_MODEL), row),
                  pl.BlockSpec((1, D_MODEL), fixed),
                  pl.BlockSpec(w_bf.shape, fixed),
                  pl.BlockSpec((1, GROUP_W), fixed),
                  pl.BlockSpec((tm, GROUP_W), tab),
                  pl.BlockSpec((tm, GROUP_W), tab)],
        out_specs=[ospec] * 5 + [pl.BlockSpec((tm * HEADS, HEAD_W), row)] * 2 + [ospec] * 3,
        out_shape=[f32o] * 5 + [jax.ShapeDtypeStruct((T * HEADS, HEAD_W), F32)] * 2 + [bfo] * 3,
        compiler_params=_cparams(("parallel",)),
        name="proj",
    )(x, g, w_bf, lb, cosf, sins)


def _hgrn_consts():
    C = HG_CHUNK
    t = np.arange(C)[:, None]
    u = np.arange(C)[None, :]
    mats = [(u <= t)]
    lows, pms = [], []
    h = C // 2
    while h >= 1:
        base = (t // (2 * h)) * (2 * h)
        r = base + h - 1
        lower = t >= base + h
        m = np.where(lower, (u > r) & (u <= t), (u > t) & (u <= r))
        mats.append(m)
        lows.append(np.broadcast_to(lower, (C, HEAD_W)))
        s = np.arange(C)[None, :]
        pms.append((t // (2 * h)) == (s // (2 * h)))
        h //= 2
    mats.append(u > t)
    wall = np.concatenate(mats, axis=0).astype(np.float32)
    return (jnp.asarray(wall, BF16), jnp.asarray(np.stack(lows), F32), jnp.asarray(np.stack(pms), F32))


def _split3(x):
    a = x.astype(BF16)
    r = x - a.astype(F32)
    b = r.astype(BF16)
    c = (r - b.astype(F32)).astype(BF16)
    return a, b, c


def _hgrn_kernel(q_ref, k_ref, v_ref, lf_ref, wall_ref, low_ref, pm_ref, o_ref, st_ref, st_scr, *, nlev):
    i = pl.program_id(2)
    C = HG_CHUNK

    @pl.when(i == 0)
    def _():
        st_scr[...] = jnp.zeros_like(st_scr)

    wall = wall_ref[...]
    eye = (lax.broadcasted_iota(I32, (C, C), 0) == lax.broadcasted_iota(I32, (C, C), 1)).astype(F32)
    st = st_scr[...]
    for c in range(q_ref.shape[0] // C):
        sl = slice(c * C, (c + 1) * C)
        q, k, v, lf = q_ref[sl, :], k_ref[sl, :], v_ref[sl, :], lf_ref[sl, :]
        lcat = jnp.concatenate(_split3(lf), axis=-1)
        d3 = jnp.dot(wall, lcat, preferred_element_type=F32)
        e_all = jnp.exp(d3[:, :HEAD_W] + d3[:, HEAD_W:2 * HEAD_W] + d3[:, 2 * HEAD_W:])
        e_g = e_all[0:C]
        e_last = e_all[C - 1:C]
        e_k = e_all[(nlev + 1) * C:(nlev + 2) * C]
        vb = v.astype(BF16)
        o = lax.dot_general((q * e_g).astype(BF16), st.astype(BF16), NT, preferred_element_type=F32)
        a = eye * jnp.sum(q * k, axis=-1, keepdims=True)
        for l in range(nlev):
            e_l = e_all[(l + 1) * C:(l + 2) * C]
            low = low_ref[l]
            ql = (q * e_l * low).astype(BF16)
            kl = (k * e_l * (1.0 - low)).astype(BF16)
            a = a + lax.dot_general(ql, kl, NT, preferred_element_type=F32) * pm_ref[l]
        o_ref[sl, :] = o + jnp.dot(a.astype(BF16), vb, preferred_element_type=F32)
        kd = (k * e_k).astype(BF16)
        st = e_last * st + lax.dot_general(vb, kd, TN, preferred_element_type=F32)
    st_scr[...] = st

    @pl.when(i == pl.num_programs(2) - 1)
    def _():
        st_ref[0, 0] = st


def _hgrn_prompt(hq, hk, hi, lf, B, L, lb_rows):
    wall, low, pm = _hgrn_consts()
    nlev = low.shape[0]
    nblk = L // lb_rows
    blk = pl.BlockSpec((lb_rows, HEAD_W), lambda b, h, i: (b * nblk + i, h))
    cst = lambda a: pl.BlockSpec(a.shape, lambda b, h, i: (0,) * a.ndim)
    return pl.pallas_call(
        functools.partial(_hgrn_kernel, nlev=nlev),
        grid=(B, HEADS, nblk),
        in_specs=[blk, blk, blk, blk, cst(wall), cst(low), cst(pm)],
        out_specs=[blk, pl.BlockSpec((1, 1, HEAD_W, HEAD_W), lambda b, h, i: (b, h, 0, 0))],
        out_shape=[jax.ShapeDtypeStruct(hq.shape, F32),
                   jax.ShapeDtypeStruct((B, HEADS, HEAD_W, HEAD_W), F32)],
        scratch_shapes=[pltpu.VMEM((HEAD_W, HEAD_W), F32)],
        compiler_params=_cparams(("parallel", "parallel", "arbitrary")),
        name="hgrn_prompt",
    )(hq, hk, hi, lf, wall, low, pm)


def _hgrn_step_kernel(qc_ref, kc_ref, lfc_ref, v_ref, s0_ref, o_ref, s_ref):
    qc, kc = qc_ref[0, 0], kc_ref[0, 0]
    dec = jnp.exp(lfc_ref[0, 0])
    v = v_ref[0, 0]
    s0 = s0_ref[0, 0]
    s_ref[0, 0] = dec * s0 + kc * v
    o_ref[0, 0] = (jnp.sum((qc * dec) * s0, axis=0, keepdims=True)
                   + jnp.sum(qc * kc, axis=0, keepdims=True) * v)


def _hgrn_step(hq, hk, hi, lf, s0):
    B = hq.shape[0]
    colv = lambda a: a.reshape(B, HEADS, HEAD_W, 1)
    cspec = pl.BlockSpec((1, 1, HEAD_W, 1), lambda b, h: (b, h, 0, 0))
    rspec = pl.BlockSpec((1, 1, 1, HEAD_W), lambda b, h: (b, h, 0, 0))
    sspec = pl.BlockSpec((1, 1, HEAD_W, HEAD_W), lambda b, h: (b, h, 0, 0))
    o, s = pl.pallas_call(
        _hgrn_step_kernel,
        grid=(B, HEADS),
        in_specs=[cspec, cspec, cspec, rspec, sspec],
        out_specs=[rspec, sspec],
        out_shape=[jax.ShapeDtypeStruct((B, HEADS, 1, HEAD_W), F32),
                   jax.ShapeDtypeStruct((B, HEADS, HEAD_W, HEAD_W), F32)],
        compiler_params=_cparams(("parallel", "parallel")),
        name="hgrn_step",
    )(colv(hq), colv(hk), colv(lf), hi.reshape(B, HEADS, 1, HEAD_W), s0)
    return o.reshape(B, GROUP_W), s


def _attn_kernel(lam_ref, q_ref, k_ref, v_ref, o_ref, qs_scr, m_scr, l_scr, acc_scr):
    qi = pl.program_id(1)
    ki = pl.program_id(2)
    bq = q_ref.shape[0]
    bk = k_ref.shape[0]

    @pl.when(ki == 0)
    def _():
        lane = lax.broadcasted_iota(I32, (bq, HEAD_W), 1)
        for h in range(HEADS):
            q = q_ref[:, h * HEAD_W:(h + 1) * HEAD_W]
            zero = jnp.zeros_like(q)
            qs_scr[h] = jnp.concatenate([jnp.where(lane < DA_DH, q, zero),
                                         jnp.where(lane >= DA_DH, q, zero)], axis=0)
        m_scr[...] = jnp.full(m_scr.shape, -jnp.inf, F32)
        l_scr[...] = jnp.zeros_like(l_scr)
        acc_scr[...] = jnp.zeros_like(acc_scr)

    @pl.when(ki <= qi)
    def _():
        row = lax.broadcasted_iota(I32, (2 * bq, bk), 0) % bq
        colv = lax.broadcasted_iota(I32, (2 * bq, bk), 1)
        keep = (colv <= row) | (ki < qi)
        for h in range(HEADS):
            kh = k_ref[:, h * HEAD_W:(h + 1) * HEAD_W]
            vh = v_ref[:, h * HEAD_W:(h + 1) * HEAD_W]
            s = lax.dot_general(qs_scr[h], kh, NT, preferred_element_type=F32)
            s = jnp.where(keep, s, -jnp.inf)
            m_old = m_scr[h]
            m_new = jnp.maximum(m_old, _row_reduce(jnp.maximum, jnp.max, s))
            alpha = jnp.exp(m_old - m_new)
            p = jnp.exp(s - m_new)
            l_scr[h] = alpha * l_scr[h] + _row_reduce(jnp.add, jnp.sum, p)
            acc_scr[h] = alpha * acc_scr[h] + jnp.dot(p.astype(BF16), vh, preferred_element_type=F32)
            m_scr[h] = m_new

    @pl.when(ki == qi)
    def _():
        lam = lam_ref[0]
        for h in range(HEADS):
            r = acc_scr[h] / l_scr[h]
            o_ref[:, h * HEAD_W:(h + 1) * HEAD_W] = r[:bq] - lam * r[bq:]


def _attn_prompt(lam, qb, kb, vb, B, L, bq):
    nq = L // bq
    qspec = pl.BlockSpec((bq, GROUP_W), lambda b, qi, ki: (b * nq + qi, 0))
    kspec = pl.BlockSpec((bq, GROUP_W), lambda b, qi, ki: (b * nq + jnp.minimum(ki, qi), 0))
    return pl.pallas_call(
        _attn_kernel,
        grid=(B, nq, nq),
        in_specs=[pl.BlockSpec(memory_space=pltpu.SMEM), qspec, kspec, kspec],
        out_specs=qspec,
        out_shape=jax.ShapeDtypeStruct(qb.shape, F32),
        scratch_shapes=[pltpu.VMEM((HEADS, 2 * bq, HEAD_W), BF16),
                        pltpu.VMEM((HEADS, 2 * bq, 1), F32),
                        pltpu.VMEM((HEADS, 2 * bq, 1), F32),
                        pltpu.VMEM((HEADS, 2 * bq, HEAD_W), F32)],
        compiler_params=_cparams(("parallel", "parallel", "arbitrary")),
        name="attn_prompt",
    )(lam, qb, kb, vb)


def _decode_kernel(pt_ref, lam_ref, q_ref, kn_ref, vn_ref, *rest):
    npg = PAGES_PER_STEP
    k_refs, v_refs = rest[:npg], rest[npg:2 * npg]
    o_ref, kcat, vcat, m_scr, l_scr, acc_scr = rest[2 * npg:]
    i = pl.program_id(1)
    nrow = 2 * HEADS

    rowi = lax.broadcasted_iota(I32, (nrow, GROUP_W), 0)
    lane = lax.broadcasted_iota(I32, (nrow, GROUP_W), 1)
    sel = (lane // HEAD_W == rowi // 2) & ((lane % HEAD_W) // DA_DH == rowi % 2)
    qbd = jnp.where(sel, jnp.broadcast_to(q_ref[0], (nrow, GROUP_W)), 0.0)

    @pl.when(i == 0)
    def _():
        m_scr[...] = jnp.full(m_scr.shape, -jnp.inf, F32)
        l_scr[...] = jnp.zeros_like(l_scr)
        acc_scr[...] = jnp.zeros_like(acc_scr)

    for p in range(npg):
        for h in range(HEADS):
            rows = slice(p * PAGE, (p + 1) * PAGE)
            cols = slice(h * HEAD_W, (h + 1) * HEAD_W)
            kcat[rows, cols] = k_refs[p][0, pl.ds(h, PAGE, stride=HEADS), :].astype(BF16)
            vcat[rows, cols] = v_refs[p][0, pl.ds(h, PAGE, stride=HEADS), :].astype(BF16)
    s = lax.dot_general(qbd.astype(BF16), kcat[...], NT, preferred_element_type=F32)
    m_old = m_scr[...]
    m_new = jnp.maximum(m_old, _row_reduce(jnp.maximum, jnp.max, s))
    alpha = jnp.exp(m_old - m_new)
    p_ = jnp.exp(s - m_new)
    l_scr[...] = alpha * l_scr[...] + _row_reduce(jnp.add, jnp.sum, p_)
    acc_scr[...] = alpha * acc_scr[...] + jnp.dot(p_.astype(BF16), vcat[...], preferred_element_type=F32)
    m_scr[...] = m_new

    @pl.when(i == pl.num_programs(1) - 1)
    def _():
        s_new = jnp.sum(qbd * kn_ref[0], axis=-1, keepdims=True)
        m_o = m_scr[...]
        m_n = jnp.maximum(m_o, s_new)
        al = jnp.exp(m_o - m_n)
        pn = jnp.exp(s_new - m_n)
        l_f = al * l_scr[...] + pn
        r = (al * acc_scr[...] + pn * vn_ref[0]) / l_f
        lam = lam_ref[0]
        outs = []
        for h in range(HEADS):
            blk = slice(h * HEAD_W, (h + 1) * HEAD_W)
            outs.append(r[2 * h:2 * h + 1, blk] - lam * r[2 * h + 1:2 * h + 2, blk])
        o_ref[0] = jnp.concatenate(outs, axis=-1)


def _attn_decode(page_table, lam, q, kn, vn, ck, cv):
    B = q.shape[0]
    nsteps = page_table.shape[1] // PAGES_PER_STEP
    tok = pl.BlockSpec((1, 1, GROUP_W), lambda b, i, pt: (b, 0, 0))

    def page(p):
        return pl.BlockSpec((1, PAGE * HEADS, HEAD_W),
                            lambda b, i, pt: (pt[b, i * PAGES_PER_STEP + p], 0, 0))

    pages = [page(p) for p in range(PAGES_PER_STEP)]
    grid_spec = pltpu.PrefetchScalarGridSpec(
        num_scalar_prefetch=1,
        grid=(B, nsteps),
        in_specs=[pl.BlockSpec(memory_space=pltpu.SMEM), tok, tok, tok] + pages + pages,
        out_specs=tok,
        scratch_shapes=[pltpu.VMEM((PAGES_PER_STEP * PAGE, GROUP_W), BF16),
                        pltpu.VMEM((PAGES_PER_STEP * PAGE, GROUP_W), BF16),
                        pltpu.VMEM((2 * HEADS, 1), F32),
                        pltpu.VMEM((2 * HEADS, 1), F32),
                        pltpu.VMEM((2 * HEADS, GROUP_W), F32)])
    r3 = lambda a: a.reshape(B, 1, GROUP_W)
    out = pl.pallas_call(
        _decode_kernel,
        grid_spec=grid_spec,
        out_shape=jax.ShapeDtypeStruct((B, 1, GROUP_W), F32),
        compiler_params=_cparams(("parallel", "arbitrary")),
        name="attn_decode",
    )(page_table, lam, r3(q), r3(kn), r3(vn), *([ck] * PAGES_PER_STEP), *([cv] * PAGES_PER_STEP))
    return out.reshape(B, GROUP_W)


def _mix_kernel(oh_ref, gate_ref, od_ref, x_ref, gh_ref, gd_ref, wo_ref, gf_ref, wq_ref,
                hp_ref, hn_ref, qh_ref, *, dscale):
    parts = []
    for h in range(HEADS):
        blk = slice(h * HEAD_W, (h + 1) * HEAD_W)
        parts.append(_rms(oh_ref[:, blk], gh_ref[...]) * gate_ref[:, blk])
    for h in range(HEADS):
        blk = slice(h * HEAD_W, (h + 1) * HEAD_W)
        parts.append(_rms(od_ref[:, blk], gd_ref[...]) * dscale)
    y = jnp.concatenate(parts, axis=-1).astype(BF16)
    hp = x_ref[...] + jnp.dot(y, wo_ref[...], preferred_element_type=F32)
    hp_ref[...] = hp
    hn = _rms(hp, gf_ref[...])
    hn_ref[...] = hn
    qh_ref[...] = jnp.dot(hn.astype(BF16), wq_ref[...], preferred_element_type=F32).astype(BF16)


def _mix(oh, gate, od, x, gh, gd, wo_bf, gf, wq_bf, dscale, tm):
    T = x.shape[0]
    row = lambda i: (i, 0)
    fixed = lambda i: (0, 0)
    half = pl.BlockSpec((tm, GROUP_W), row)
    full = pl.BlockSpec((tm, D_MODEL), row)
    cst = lambda a: pl.BlockSpec(a.shape, fixed)
    nq = wq_bf.shape[1]
    return pl.pallas_call(
        functools.partial(_mix_kernel, dscale=dscale),
        grid=(T // tm,),
        in_specs=[half, half, half, full, cst(gh), cst(gd), cst(wo_bf), cst(gf), cst(wq_bf)],
        out_specs=[full, full, pl.BlockSpec((tm, nq), row)],
        out_shape=[jax.ShapeDtypeStruct((T, D_MODEL), F32), jax.ShapeDtypeStruct((T, D_MODEL), F32),
                   jax.ShapeDtypeStruct((T, nq), BF16)],
        compiler_params=_cparams(("parallel",)),
        name="mix",
    )(oh, gate, od, x, gh, gd, wo_bf, gf, wq_bf)


def _staircase():
    K = PEER_TOPK
    return [(a, b) for a in range(K) for b in range(K) if (a + 1) * (b + 1) <= K]


def _topk_kernel(qh_ref, sk_ref, e_ref, g_ref, s_scr, val_scr, idx_scr, cand_scr, cidx_scr, sc_scr, e_scr):
    K = PEER_TOPK
    NK = PEER_NKEYS
    tb = qh_ref.shape[0]
    half_w = PEER_HEADS * NK
    neg = -jnp.inf

    for c in range(2):
        qc = qh_ref[:, c * half_w:(c + 1) * half_w]
        s = lax.dot_general(sk_ref[c], qc, NT, preferred_element_type=F32)
        s_scr[...] = s.reshape(NK, PEER_HEADS, tb)

        def body(a, carry):
            sv = [s_scr[n] for n in range(NK)]
            m = _tree(jnp.maximum, sv)
            idx = _tree(jnp.minimum, [jnp.where(sv[n] == m, float(n), float(NK)) for n in range(NK)])
            for n in range(NK):
                s_scr[n] = jnp.where(idx == float(n), neg, sv[n])
            val_scr[c, a] = m
            idx_scr[c, a] = idx.astype(I32)
            return carry

        lax.fori_loop(0, K, body, 0)

    pairs = _staircase()
    for i, (a, b) in enumerate(pairs):
        cand_scr[i] = val_scr[0, a] + val_scr[1, b]
        cidx_scr[i] = (idx_scr[0, a] * NK + idx_scr[1, b]) * WORD_ROWS
    flats = [float(a * K + b) for a, b in pairs]
    big = float(K * K)

    def body2(r, carry):
        cs = [cand_scr[i] for i in range(len(pairs))]
        m = _tree(jnp.maximum, cs)
        pos = _tree(jnp.minimum, [jnp.where(cv == m, fl, big) for cv, fl in zip(cs, flats)])
        picks = []
        for i, (cv, fl) in enumerate(zip(cs, flats)):
            hit = pos == fl
            cand_scr[i] = jnp.where(hit, neg, cv)
            picks.append(jnp.where(hit, cidx_scr[i], 0))
        sc_scr[r] = m
        e_scr[r] = _tree(jnp.maximum, picks)
        return carry

    lax.fori_loop(0, K, body2, 0)
    sc = sc_scr[...]
    ex = jnp.exp(sc - sc[0:1])
    g = ex / jnp.sum(ex, axis=0, keepdims=True)
    g_ref[...] = g.reshape(PEER_SEL, tb).T
    e_ref[...] = e_scr[...].reshape(PEER_SEL, tb).T


def _topk(qh, sk_big, tb):
    T = qh.shape[0]
    ncand = len(_staircase())
    hw = (PEER_HEADS, tb)
    return pl.pallas_call(
        _topk_kernel,
        grid=(T // tb,),
        in_specs=[pl.BlockSpec((tb, qh.shape[1]), lambda i: (i, 0)),
                  pl.BlockSpec(sk_big.shape, lambda i: (0, 0, 0))],
        out_specs=[pl.BlockSpec((tb, PEER_SEL), lambda i: (i, 0))] * 2,
        out_shape=[jax.ShapeDtypeStruct((T, PEER_SEL), I32),
                   jax.ShapeDtypeStruct((T, PEER_SEL), F32)],
        scratch_shapes=[pltpu.VMEM((PEER_NKEYS,) + hw, F32),
                        pltpu.VMEM((2, PEER_TOPK) + hw, F32),
                        pltpu.VMEM((2, PEER_TOPK) + hw, I32),
                        pltpu.VMEM((ncand,) + hw, F32),
                        pltpu.VMEM((ncand,) + hw, I32),
                        pltpu.VMEM((PEER_TOPK,) + hw, F32),
                        pltpu.VMEM((PEER_TOPK,) + hw, I32)],
        compiler_params=_cparams(("parallel",)),
        name="peer_topk",
    )(qh, sk_big)


def _pack_kernel(t_ref, o_ref):
    n = t_ref.shape[0]

    def bits(x):
        return pltpu.bitcast(x.astype(BF16).astype(F32), jnp.uint32) >> 16

    for s in range(WORD_ROWS):
        lo = bits(t_ref[:, (2 * s) * HEAD_W:(2 * s + 1) * HEAD_W])
        hi = bits(t_ref[:, (2 * s + 1) * HEAD_W:(2 * s + 2) * HEAD_W])
        o_ref[pl.ds(s, n, stride=WORD_ROWS), :] = pltpu.bitcast((hi << 16) | lo, I32)


def _pack_table(tab, rows=256):
    n = tab.shape[0]
    return pl.pallas_call(
        _pack_kernel,
        grid=(n // rows,),
        in_specs=[pl.BlockSpec((rows, D_MODEL), lambda i: (i, 0))],
        out_specs=pl.BlockSpec((rows * WORD_ROWS, HEAD_W), lambda i: (i, 0)),
        out_shape=jax.ShapeDtypeStruct((n * WORD_ROWS, HEAD_W), I32),
        compiler_params=_cparams(("parallel",)),
        name="pack_table",
    )(tab)


def _gather_rows(idx_ref, tab_ref, stg, t):
    for j in range(PEER_SEL):
        r = pl.multiple_of(idx_ref[t, j], WORD_ROWS)
        stg[j * WORD_ROWS:(j + 1) * WORD_ROWS, :] = tab_ref[pl.ds(r, WORD_ROWS), :]


def _peer_u_kernel(idx_ref, x_ref, g_ref, tab_ref, dmask_ref, gsum_ref, expand_ref, w_ref, stg, c_scr):
    tb = x_ref.shape[0]
    dmask = dmask_ref[...]

    def group(gi, carry):
        rows = []
        for i in range(8):
            t = gi * 8 + i
            buf = stg.at[i % 2]
            _gather_rows(idx_ref, tab_ref, buf, t)
            ub = pltpu.bitcast(buf[...], BF16)
            xt = x_ref[t].astype(BF16)
            r = lax.dot_general(xt, ub, NT, preferred_element_type=F32)
            rows.append(jnp.sum(r * dmask, axis=0, keepdims=True))
        c_scr[pl.ds(pl.multiple_of(gi * 8, 8), 8), :] = jnp.concatenate(rows, axis=0)
        return carry

    lax.fori_loop(0, tb // 8, group, 0)
    c = c_scr[...]
    c_hi = c.astype(BF16)
    c_lo = (c - c_hi.astype(F32)).astype(BF16)
    a = (jnp.dot(c_hi, gsum_ref[...], preferred_element_type=F32)
         + jnp.dot(c_lo, gsum_ref[...], preferred_element_type=F32))
    w = g_ref[...] * jax.nn.gelu(a)
    w_ref[...] = jnp.dot(w.astype(BF16), expand_ref[...], preferred_element_type=F32)


def _peer_v_kernel(idx_ref, w_ref, tab_ref, dmask_ref, o_ref, stg):
    tb = w_ref.shape[0]
    dmask = dmask_ref[...]

    def group(gi, carry):
        w8 = w_ref[pl.ds(pl.multiple_of(gi * 8, 8), 8), :]
        for i in range(8):
            t = gi * 8 + i
            buf = stg.at[i % 2]
            _gather_rows(idx_ref, tab_ref, buf, t)
            vb = pltpu.bitcast(buf[...], BF16)
            lhs = (jnp.broadcast_to(w8[i:i + 1, :], dmask.shape) * dmask).astype(BF16)
            o_ref[t] = jnp.dot(lhs, vb, preferred_element_type=F32)
        return carry

    lax.fori_loop(0, tb // 8, group, 0)


def _peer_consts():
    sel_w = PEER_SEL * ROW_CHUNKS
    lane = np.arange(sel_w)
    dmask = (lane[None, :] % ROW_CHUNKS == np.arange(ROW_CHUNKS)[:, None]).astype(np.float32)
    gsum = (lane[:, None] // ROW_CHUNKS == np.arange(PEER_SEL)[None, :]).astype(np.float32)
    return jnp.asarray(dmask, F32), jnp.asarray(gsum, BF16), jnp.asarray(gsum.T, BF16)


def _peer_u(idx, hn, g, tab_u, tb):
    T = hn.shape[0]
    dmask, gsum, expand = _peer_consts()
    sel_w = PEER_SEL * ROW_CHUNKS
    cst = lambda a: pl.BlockSpec(a.shape, lambda i: (0, 0))
    return pl.pallas_call(
        _peer_u_kernel,
        grid=(T // tb,),
        in_specs=[pl.BlockSpec((tb, PEER_SEL), lambda i: (i, 0), memory_space=pltpu.SMEM),
                  pl.BlockSpec((tb, ROW_CHUNKS, HEAD_W), lambda i: (i, 0, 0)),
                  pl.BlockSpec((tb, PEER_SEL), lambda i: (i, 0)),
                  pl.BlockSpec(memory_space=pltpu.VMEM),
                  cst(dmask), cst(gsum), cst(expand)],
        out_specs=pl.BlockSpec((tb, sel_w), lambda i: (i, 0)),
        out_shape=jax.ShapeDtypeStruct((T, sel_w), F32),
        scratch_shapes=[pltpu.VMEM((2, PEER_SEL * WORD_ROWS, HEAD_W), I32),
                        pltpu.VMEM((tb, sel_w), F32)],
        compiler_params=_cparams(("arbitrary",)),
        name="peer_u",
    )(idx, hn.reshape(T, ROW_CHUNKS, HEAD_W), g, tab_u, dmask, gsum, expand)


def _peer_v(idx, wexp, tab_v, tb):
    T = wexp.shape[0]
    dmask, _, _ = _peer_consts()
    out = pl.pallas_call(
        _peer_v_kernel,
        grid=(T // tb,),
        in_specs=[pl.BlockSpec((tb, PEER_SEL), lambda i: (i, 0), memory_space=pltpu.SMEM),
                  pl.BlockSpec((tb, wexp.shape[1]), lambda i: (i, 0)),
                  pl.BlockSpec(memory_space=pltpu.VMEM),
                  pl.BlockSpec(dmask.shape, lambda i: (0, 0))],
        out_specs=pl.BlockSpec((tb, ROW_CHUNKS, HEAD_W), lambda i: (i, 0, 0)),
        out_shape=jax.ShapeDtypeStruct((T, ROW_CHUNKS, HEAD_W), F32),
        scratch_shapes=[pltpu.VMEM((2, PEER_SEL * WORD_ROWS, HEAD_W), I32)],
        compiler_params=_cparams(("arbitrary",)),
        name="peer_v",
    )(idx, wexp, tab_v, dmask)
    return out.reshape(T, D_MODEL)


def _ple_kernel(hp_ref, pe_ref, p_ref, wg_ref, bg_ref, wp_ref, gf_ref, y_ref):
    h = hp_ref[...] + pe_ref[...]
    gate = jax.nn.sigmoid(jnp.dot(h.astype(BF16), wg_ref[...], preferred_element_type=F32) + bg_ref[...])
    h = h + gate * jnp.dot(p_ref[...].astype(BF16), wp_ref[...], preferred_element_type=F32)
    y_ref[...] = _rms(h, gf_ref[...])


def _ple(hp, pe, p, wg_bf, bg, wp_bf, gf, tm):
    T = hp.shape[0]
    row = lambda i: (i, 0)
    full = pl.BlockSpec((tm, D_MODEL), row)
    cst = lambda a: pl.BlockSpec(a.shape, lambda i: (0, 0))
    return pl.pallas_call(
        _ple_kernel,
        grid=(T // tm,),
        in_specs=[full, full, pl.BlockSpec((tm, PLE_DIM), row), cst(wg_bf), cst(bg), cst(wp_bf), cst(gf)],
        out_specs=full,
        out_shape=jax.ShapeDtypeStruct((T, D_MODEL), F32),
        compiler_params=_cparams(("parallel",)),
        name="ple",
    )(hp, pe, p, wg_bf, bg, wp_bf, gf)


def _rope_tables(pos):
    half = DA_DH // 2
    freqs = ROPE_THETA ** (-jnp.arange(half, dtype=F32) / half)
    ang = pos.astype(F32)[:, None] * freqs[None, :]
    cos = jnp.tile(jnp.cos(ang), (1, GROUP_W // half))
    sign = jnp.where((jnp.arange(GROUP_W) % DA_DH) < half, -1.0, 1.0).astype(F32)
    sin = jnp.tile(jnp.sin(ang), (1, GROUP_W // half)) * sign[None, :]
    return cos, sin


def _ffn(hp_parts, p, weights, tb_peer):
    hp, hn, qh = hp_parts
    (sk_big, tab_u, tab_v, wg_bf, bg, wp_bf, gfinal) = weights
    T = hp.shape[0]
    tk = 128
    tpad = -(-T // tk) * tk
    qh_p = jnp.pad(qh, ((0, tpad - T), (0, 0))) if tpad != T else qh
    idx, gw = _topk(qh_p, sk_big, tk)
    idx, gw = idx[:T], gw[:T]
    wexp = _peer_u(idx, hn, gw, tab_u, tb_peer)
    pe = _peer_v(idx, wexp, tab_v, tb_peer)
    tm = min(256, T)
    return _ple(hp, pe, p, wg_bf, bg, wp_bf, gfinal, tm)


def kernel(x_prompt, x_sample, p_prompt, p_sample, cache_k, cache_v, state_hgrn, page_table, g_attn, w_in, hgrn_gamma, g_hgrn_norm, lambda_q1, lambda_k1, lambda_q2, lambda_k2, g_diff_norm, w_out, g_ffn, peer_w_query, peer_sub_keys, peer_u, peer_v, ple_w_gate, ple_b_gate, ple_w_proj, g_final):
    Bp, Lp, D = x_prompt.shape
    Bs, Ls, _ = x_sample.shape
    assert D == D_MODEL and Ls == 1 and w_in.shape[0] == 1
    l = 0
    past_len = page_table.shape[1] * cache_k.shape[2]
    lam_init = 0.8 - 0.6 * math.exp(-0.3 * l)
    lam = (jnp.exp(jnp.sum(lambda_q1[l] * lambda_k1[l])) - jnp.exp(jnp.sum(lambda_q2[l] * lambda_k2[l]))
           + lam_init).reshape(1).astype(F32)
    lb = jnp.cumsum(jax.nn.softmax(hgrn_gamma.astype(F32), axis=0), axis=0)[l].reshape(1, GROUP_W)

    w_in_bf = w_in[l].astype(BF16)
    w_out_bf = w_out[l].astype(BF16)
    nqc = PEER_HEADS * 2 * PEER_NKEYS
    wq_bf = (peer_w_query[l].reshape(D, PEER_HEADS, 2, PEER_NKEYS).transpose(0, 2, 1, 3)
             .reshape(D, nqc).astype(BF16))
    sk_big = jnp.einsum('hcnk,hg->cnhgk', peer_sub_keys[l], jnp.eye(PEER_HEADS, dtype=F32)).reshape(
        2, PEER_NKEYS * PEER_HEADS, PEER_HEADS * PEER_NKEYS).astype(BF16)
    tab_u = _pack_table(peer_u[l])
    tab_v = _pack_table(peer_v[l])
    wg_bf = ple_w_gate[l].astype(BF16)
    wp_bf = ple_w_proj[l].astype(BF16)
    row = lambda a: a.reshape(1, -1).astype(F32)
    ffn_w = (sk_big, tab_u, tab_v, wg_bf, row(ple_b_gate[l]), wp_bf, row(g_final))

    def group(x, pos_tab, tm):
        cosf, sins = pos_tab
        return _proj(x, row(g_attn[l]), w_in_bf, lb, cosf, sins, tm)

    def mix(oh, gate, od, x, tm):
        return _mix(oh, gate, od, x, row(g_hgrn_norm[l]), row(g_diff_norm[l]), w_out_bf, row(g_ffn[l]),
                    wq_bf, 1.0 - lam_init, tm)

    Tp = Bp * Lp
    xp = x_prompt.reshape(Tp, D)
    tm_p = math.gcd(Lp, 512)
    hq, hk, lf, hi, gate, k_p, v_p, qb, kb, vb = group(xp, _rope_tables(jnp.arange(Lp)), tm_p)
    o_h, st_p = _hgrn_prompt(hq, hk, hi, lf, Bp, Lp, math.gcd(Lp, 512))
    o_d = _attn_prompt(lam, qb, kb, vb, Bp, Lp, math.gcd(Lp, 512))
    y_p = _ffn(mix(o_h, gate, o_d, xp, min(256, Tp)), p_prompt.reshape(Tp, PLE_DIM), ffn_w, min(64, Tp))

    xs = x_sample.reshape(Bs, D)
    pos_s = jnp.full((Bs,), past_len, dtype=jnp.int32)
    hq, hk, lf, hi, gate, k_s, v_s, qb, kb, vb = group(xs, _rope_tables(pos_s), Bs)
    o_h, st_s = _hgrn_step(hq, hk, hi, lf, state_hgrn.reshape(Bs, HEADS, HEAD_W, HEAD_W))
    npool = cache_k.shape[1]
    o_d = _attn_decode(page_table, lam, qb.astype(F32), k_s.reshape(Bs, GROUP_W), v_s.reshape(Bs, GROUP_W),
                       cache_k.reshape(npool, PAGE * HEADS, HEAD_W),
                       cache_v.reshape(npool, PAGE * HEADS, HEAD_W))
    y_s = _ffn(mix(o_h, gate, o_d, xs, Bs), p_sample.reshape(Bs, PLE_DIM), ffn_w, Bs)

    hd = (HEADS, HEAD_W)
    return (y_p.reshape(Bp, Lp, D), y_s.reshape(Bs, Ls, D),
            k_p.reshape((1, Bp, Lp) + hd), v_p.reshape((1, Bp, Lp) + hd),
            jnp.swapaxes(st_p, -1, -2)[None],
            k_s.reshape((1, Bs, Ls) + hd), v_s.reshape((1, Bs, Ls) + hd), st_s[None])
```

```python
import functools
import math

import numpy as np
import jax
import jax.numpy as jnp
from jax import lax
from jax.experimental import pallas as pl
from jax.experimental.pallas import tpu as pltpu

F32 = jnp.float32
BF16 = jnp.bfloat16
I32 = jnp.int32

D_MODEL = 1024
HEADS = 4
HEAD_W = 128
GROUP_W = HEADS * HEAD_W
DA_DH = 64
ROPE_THETA = 10000.0
HG_CHUNK = 64
PEER_HEADS = 8
PEER_NKEYS = 128
PEER_TOPK = 16
PEER_SEL = PEER_HEADS * PEER_TOPK
PLE_DIM = 256
EPS = 1e-6
PAGE = 128
PAGES_PER_STEP = 8
ROW_CHUNKS = D_MODEL // HEAD_W
WORD_ROWS = ROW_CHUNKS // 2
VMEM_LIMIT = 56 * 1024 * 1024

NT = (((1,), (1,)), ((), ()))
TN = (((0,), (0,)), ((), ()))


def _cparams(sem):
    return pltpu.CompilerParams(dimension_semantics=sem, vmem_limit_bytes=VMEM_LIMIT)


def _tree(op, xs):
    xs = list(xs)
    while len(xs) > 1:
        nxt = [op(xs[i], xs[i + 1]) for i in range(0, len(xs) - 1, 2)]
        if len(xs) % 2:
            nxt.append(xs[-1])
        xs = nxt
    return xs[0]


def _row_reduce(op, lane_reduce, x):
    blocks = [x[:, i * HEAD_W:(i + 1) * HEAD_W] for i in range(x.shape[1] // HEAD_W)]
    return lane_reduce(_tree(op, blocks), axis=-1, keepdims=True)


def _rms(x, g):
    return x * lax.rsqrt(jnp.mean(x * x, axis=-1, keepdims=True) + EPS) * g


def _store_head_rows(ref, x):
    n = x.shape[0]
    for h in range(HEADS):
        ref[pl.ds(h, n, stride=HEADS), :] = x[:, h * HEAD_W:(h + 1) * HEAD_W]


def _proj_kernel(x_ref, g_ref, w_ref, lb_ref, cos_ref, sin_ref,
                 hq_ref, hk_ref, lf_ref, hi_ref, gate_ref, k_ref, v_ref,
                 qb_ref, kb_ref, vb_ref):
    xb = _rms(x_ref[...], g_ref[...]).astype(BF16)

    def col(i):
        return jnp.dot(xb, w_ref[:, i * GROUP_W:(i + 1) * GROUP_W], preferred_element_type=F32)

    hq_ref[...] = col(0)
    lb = lb_ref[...]
    f = lb + (1.0 - lb) * jax.nn.sigmoid(col(1))
    lf_ref[...] = jnp.log(f)
    hk_ref[...] = 1.0 - f
    hi_ref[...] = col(2)
    gate_ref[...] = jax.nn.silu(col(3))

    cosf = cos_ref[...]
    sins = sin_ref[...]
    lane = lax.broadcasted_iota(I32, cosf.shape, 1)
    first_half = (lane % DA_DH) < (DA_DH // 2)

    def rope(x):
        swapped = jnp.where(first_half,
                            pltpu.roll(x, GROUP_W - DA_DH // 2, 1),
                            pltpu.roll(x, DA_DH // 2, 1))
        return x * cosf + swapped * sins

    q = rope(col(4))
    qb_ref[...] = (q * (DA_DH ** -0.5)).astype(BF16)
    k = rope(col(5))
    _store_head_rows(k_ref, k)
    kb_ref[...] = k.astype(BF16)
    v = col(6)
    _store_head_rows(v_ref, v)
    vb_ref[...] = v.astype(BF16)


def _proj(x, g, w_bf, lb, cosf, sins, tm):
    T = x.shape[0]
    nl = cosf.shape[0] // tm
    row = lambda i: (i, 0)
    fixed = lambda i: (0, 0)
    tab = lambda i: (i % nl, 0)
    f32o = jax.ShapeDtypeStruct((T, GROUP_W), F32)
    bfo = jax.ShapeDtypeStruct((T, GROUP_W), BF16)
    ospec = pl.BlockSpec((tm, GROUP_W), row)
    return pl.pallas_call(
        _proj_kernel,
        grid=(T // tm,),
        in_specs=[pl.BlockSpec((tm, D_MODEL), row),
                  pl.BlockSpec((1, D_MODEL), fixed),
                  pl.BlockSpec(w_bf.shape, fixed),
                  pl.BlockSpec((1, GROUP_W), fixed),
                  pl.BlockSpec((tm, GROUP_W), tab),
                  pl.BlockSpec((tm, GROUP_W), tab)],
        out_specs=[ospec] * 5 + [pl.BlockSpec((tm * HEADS, HEAD_W), row)] * 2 + [ospec] * 3,
        out_shape=[f32o] * 5 + [jax.ShapeDtypeStruct((T * HEADS, HEAD_W), F32)] * 2 + [bfo] * 3,
        compiler_params=_cparams(("parallel",)),
        name="proj",
    )(x, g, w_bf, lb, cosf, sins)


def _hgrn_consts():
    C = HG_CHUNK
    t = np.arange(C)[:, None]
    u = np.arange(C)[None, :]
    mats = [(u <= t)]
    lows, pms = [], []
    h = C // 2
    while h >= 1:
        base = (t // (2 * h)) * (2 * h)
        r = base + h - 1
        lower = t >= base + h
        m = np.where(lower, (u > r) & (u <= t), (u > t) & (u <= r))
        mats.append(m)
        lows.append(np.broadcast_to(lower, (C, HEAD_W)))
        s = np.arange(C)[None, :]
        pms.append((t // (2 * h)) == (s // (2 * h)))
        h //= 2
    mats.append(u > t)
    wall = np.concatenate(mats, axis=0).astype(np.float32)
    return (jnp.asarray(wall, BF16), jnp.asarray(np.stack(lows), F32), jnp.asarray(np.stack(pms), F32))


def _split3(x):
    a = x.astype(BF16)
    r = x - a.astype(F32)
    b = r.astype(BF16)
    c = (r - b.astype(F32)).astype(BF16)
    return a, b, c


def _hgrn_kernel(q_ref, k_ref, v_ref, lf_ref, wall_ref, low_ref, pm_ref, o_ref, st_ref, st_scr, *, nlev):
    i = pl.program_id(2)
    C = HG_CHUNK

    @pl.when(i == 0)
    def _():
        st_scr[...] = jnp.zeros_like(st_scr)

    wall = wall_ref[...]
    eye = (lax.broadcasted_iota(I32, (C, C), 0) == lax.broadcasted_iota(I32, (C, C), 1)).astype(F32)
    st = st_scr[...]
    for c in range(q_ref.shape[0] // C):
        sl = slice(c * C, (c + 1) * C)
        q, k, v, lf = q_ref[sl, :], k_ref[sl, :], v_ref[sl, :], lf_ref[sl, :]
        lcat = jnp.concatenate(_split3(lf), axis=-1)
        d3 = jnp.dot(wall, lcat, preferred_element_type=F32)
        e_all = jnp.exp(d3[:, :HEAD_W] + d3[:, HEAD_W:2 * HEAD_W] + d3[:, 2 * HEAD_W:])
        e_g = e_all[0:C]
        e_last = e_all[C - 1:C]
        e_k = e_all[(nlev + 1) * C:(nlev + 2) * C]
        vb = v.astype(BF16)
        o = lax.dot_general((q * e_g).astype(BF16), st.astype(BF16), NT, preferred_element_type=F32)
        a = eye * jnp.sum(q * k, axis=-1, keepdims=True)
        for l in range(nlev):
            e_l = e_all[(l + 1) * C:(l + 2) * C]
            low = low_ref[l]
            ql = (q * e_l * low).astype(BF16)
            kl = (k * e_l * (1.0 - low)).astype(BF16)
            a = a + lax.dot_general(ql, kl, NT, preferred_element_type=F32) * pm_ref[l]
        o_ref[sl, :] = o + jnp.dot(a.astype(BF16), vb, preferred_element_type=F32)
        kd = (k * e_k).astype(BF16)
        st = e_last * st + lax.dot_general(vb, kd, TN, preferred_element_type=F32)
    st_scr[...] = st

    @pl.when(i == pl.num_programs(2) - 1)
    def _():
        st_ref[0, 0] = st


def _hgrn_prompt(hq, hk, hi, lf, B, L, lb_rows):
    wall, low, pm = _hgrn_consts()
    nlev = low.shape[0]
    nblk = L // lb_rows
    blk = pl.BlockSpec((lb_rows, HEAD_W), lambda b, h, i: (b * nblk + i, h))
    cst = lambda a: pl.BlockSpec(a.shape, lambda b, h, i: (0,) * a.ndim)
    return pl.pallas_call(
        functools.partial(_hgrn_kernel, nlev=nlev),
        grid=(B, HEADS, nblk),
        in_specs=[blk, blk, blk, blk, cst(wall), cst(low), cst(pm)],
        out_specs=[blk, pl.BlockSpec((1, 1, HEAD_W, HEAD_W), lambda b, h, i: (b, h, 0, 0))],
        out_shape=[jax.ShapeDtypeStruct(hq.shape, F32),
                   jax.ShapeDtypeStruct((B, HEADS, HEAD_W, HEAD_W), F32)],
        scratch_shapes=[pltpu.VMEM((HEAD_W, HEAD_W), F32)],
        compiler_params=_cparams(("parallel", "parallel", "arbitrary")),
        name="hgrn_prompt",
    )(hq, hk, hi, lf, wall, low, pm)


def _hgrn_step_kernel(qc_ref, kc_ref, lfc_ref, v_ref, s0_ref, o_ref, s_ref):
    qc, kc = qc_ref[0, 0], kc_ref[0, 0]
    dec = jnp.exp(lfc_ref[0, 0])
    v = v_ref[0, 0]
    s0 = s0_ref[0, 0]
    s_ref[0, 0] = dec * s0 + kc * v
    o_ref[0, 0] = (jnp.sum((qc * dec) * s0, axis=0, keepdims=True)
                   + jnp.sum(qc * kc, axis=0, keepdims=True) * v)


def _hgrn_step(hq, hk, hi, lf, s0):
    B = hq.shape[0]
    colv = lambda a: a.reshape(B, HEADS, HEAD_W, 1)
    cspec = pl.BlockSpec((1, 1, HEAD_W, 1), lambda b, h: (b, h, 0, 0))
    rspec = pl.BlockSpec((1, 1, 1, HEAD_W), lambda b, h: (b, h, 0, 0))
    sspec = pl.BlockSpec((1, 1, HEAD_W, HEAD_W), lambda b, h: (b, h, 0, 0))
    o, s = pl.pallas_call(
        _hgrn_step_kernel,
        grid=(B, HEADS),
        in_specs=[cspec, cspec, cspec, rspec, sspec],
        out_specs=[rspec, sspec],
        out_shape=[jax.ShapeDtypeStruct((B, HEADS, 1, HEAD_W), F32),
                   jax.ShapeDtypeStruct((B, HEADS, HEAD_W, HEAD_W), F32)],
        compiler_params=_cparams(("parallel", "parallel")),
        name="hgrn_step",
    )(colv(hq), colv(hk), colv(lf), hi.reshape(B, HEADS, 1, HEAD_W), s0)
    return o.reshape(B, GROUP_W), s


def _attn_kernel(lam_ref, q_ref, k_ref, v_ref, o_ref, qs_scr, m_scr, l_scr, acc_scr):
    qi = pl.program_id(1)
    ki = pl.program_id(2)
    bq = q_ref.shape[0]
    bk = k_ref.shape[0]

    @pl.when(ki == 0)
    def _():
        lane = lax.broadcasted_iota(I32, (bq, HEAD_W), 1)
        for h in range(HEADS):
            q = q_ref[:, h * HEAD_W:(h + 1) * HEAD_W]
            zero = jnp.zeros_like(q)
            qs_scr[h] = jnp.concatenate([jnp.where(lane < DA_DH, q, zero),
                                         jnp.where(lane >= DA_DH, q, zero)], axis=0)
        m_scr[...] = jnp.full(m_scr.shape, -jnp.inf, F32)
        l_scr[...] = jnp.zeros_like(l_scr)
        acc_scr[...] = jnp.zeros_like(acc_scr)

    @pl.when(ki <= qi)
    def _():
        row = lax.broadcasted_iota(I32, (2 * bq, bk), 0) % bq
        colv = lax.broadcasted_iota(I32, (2 * bq, bk), 1)
        keep = (colv <= row) | (ki < qi)
        for h in range(HEADS):
            kh = k_ref[:, h * HEAD_W:(h + 1) * HEAD_W]
            vh = v_ref[:, h * HEAD_W:(h + 1) * HEAD_W]
            s = lax.dot_general(qs_scr[h], kh, NT, preferred_element_type=F32)
            s = jnp.where(keep, s, -jnp.inf)
            m_old = m_scr[h]
            m_new = jnp.maximum(m_old, _row_reduce(jnp.maximum, jnp.max, s))
            alpha = jnp.exp(m_old - m_new)
            p = jnp.exp(s - m_new)
            l_scr[h] = alpha * l_scr[h] + _row_reduce(jnp.add, jnp.sum, p)
            acc_scr[h] = alpha * acc_scr[h] + jnp.dot(p.astype(BF16), vh, preferred_element_type=F32)
            m_scr[h] = m_new

    @pl.when(ki == qi)
    def _():
        lam = lam_ref[0]
        for h in range(HEADS):
            r = acc_scr[h] / l_scr[h]
            o_ref[:, h * HEAD_W:(h + 1) * HEAD_W] = r[:bq] - lam * r[bq:]


def _attn_prompt(lam, qb, kb, vb, B, L, bq):
    nq = L // bq
    qspec = pl.BlockSpec((bq, GROUP_W), lambda b, qi, ki: (b * nq + qi, 0))
    kspec = pl.BlockSpec((bq, GROUP_W), lambda b, qi, ki: (b * nq + jnp.minimum(ki, qi), 0))
    return pl.pallas_call(
        _attn_kernel,
        grid=(B, nq, nq),
        in_specs=[pl.BlockSpec(memory_space=pltpu.SMEM), qspec, kspec, kspec],
        out_specs=qspec,
        out_shape=jax.ShapeDtypeStruct(qb.shape, F32),
        scratch_shapes=[pltpu.VMEM((HEADS, 2 * bq, HEAD_W), BF16),
                        pltpu.VMEM((HEADS, 2 * bq, 1), F32),
                        pltpu.VMEM((HEADS, 2 * bq, 1), F32),
                        pltpu.VMEM((HEADS, 2 * bq, HEAD_W), F32)],
        compiler_params=_cparams(("parallel", "parallel", "arbitrary")),
        name="attn_prompt",
    )(lam, qb, kb, vb)


def _decode_kernel(pt_ref, lam_ref, q_ref, kn_ref, vn_ref, *rest):
    npg = PAGES_PER_STEP
    k_refs, v_refs = rest[:npg], rest[npg:2 * npg]
    o_ref, kcat, vcat, m_scr, l_scr, acc_scr = rest[2 * npg:]
    i = pl.program_id(1)
    nrow = 2 * HEADS

    rowi = lax.broadcasted_iota(I32, (nrow, GROUP_W), 0)
    lane = lax.broadcasted_iota(I32, (nrow, GROUP_W), 1)
    sel = (lane // HEAD_W == rowi // 2) & ((lane % HEAD_W) // DA_DH == rowi % 2)
    qbd = jnp.where(sel, jnp.broadcast_to(q_ref[0], (nrow, GROUP_W)), 0.0)

    @pl.when(i == 0)
    def _():
        m_scr[...] = jnp.full(m_scr.shape, -jnp.inf, F32)
        l_scr[...] = jnp.zeros_like(l_scr)
        acc_scr[...] = jnp.zeros_like(acc_scr)

    for p in range(npg):
        for h in range(HEADS):
            rows = slice(p * PAGE, (p + 1) * PAGE)
            cols = slice(h * HEAD_W, (h + 1) * HEAD_W)
            kcat[rows, cols] = k_refs[p][0, pl.ds(h, PAGE, stride=HEADS), :].astype(BF16)
            vcat[rows, cols] = v_refs[p][0, pl.ds(h, PAGE, stride=HEADS), :].astype(BF16)
    s = lax.dot_general(qbd.astype(BF16), kcat[...], NT, preferred_element_type=F32)
    m_old = m_scr[...]
    m_new = jnp.maximum(m_old, _row_reduce(jnp.maximum, jnp.max, s))
    alpha = jnp.exp(m_old - m_new)
    p_ = jnp.exp(s - m_new)
    l_scr[...] = alpha * l_scr[...] + _row_reduce(jnp.add, jnp.sum, p_)
    acc_scr[...] = alpha * acc_scr[...] + jnp.dot(p_.astype(BF16), vcat[...], preferred_element_type=F32)
    m_scr[...] = m_new

    @pl.when(i == pl.num_programs(1) - 1)
    def _():
        s_new = jnp.sum(qbd * kn_ref[0], axis=-1, keepdims=True)
        m_o = m_scr[...]
        m_n = jnp.maximum(m_o, s_new)
        al = jnp.exp(m_o - m_n)
        pn = jnp.exp(s_new - m_n)
        l_f = al * l_scr[...] + pn
        r = (al * acc_scr[...] + pn * vn_ref[0]) / l_f
        lam = lam_ref[0]
        outs = []
        for h in range(HEADS):
            blk = slice(h * HEAD_W, (h + 1) * HEAD_W)
            outs.append(r[2 * h:2 * h + 1, blk] - lam * r[2 * h + 1:2 * h + 2, blk])
        o_ref[0] = jnp.concatenate(outs, axis=-1)


def _attn_decode(page_table, lam, q, kn, vn, ck, cv):
    B = q.shape[0]
    nsteps = page_table.shape[1] // PAGES_PER_STEP
    tok = pl.BlockSpec((1, 1, GROUP_W), lambda b, i, pt: (b, 0, 0))

    def page(p):
        return pl.BlockSpec((1, PAGE * HEADS, HEAD_W),
                            lambda b, i, pt: (pt[b, i * PAGES_PER_STEP + p], 0, 0))

    pages = [page(p) for p in range(PAGES_PER_STEP)]
    grid_spec = pltpu.PrefetchScalarGridSpec(
        num_scalar_prefetch=1,
        grid=(B, nsteps),
        in_specs=[pl.BlockSpec(memory_space=pltpu.SMEM), tok, tok, tok] + pages + pages,
        out_specs=tok,
        scratch_shapes=[pltpu.VMEM((PAGES_PER_STEP * PAGE, GROUP_W), BF16),
                        pltpu.VMEM((PAGES_PER_STEP * PAGE, GROUP_W), BF16),
                        pltpu.VMEM((2 * HEADS, 1), F32),
                        pltpu.VMEM((2 * HEADS, 1), F32),
                        pltpu.VMEM((2 * HEADS, GROUP_W), F32)])
    r3 = lambda a: a.reshape(B, 1, GROUP_W)
    out = pl.pallas_call(
        _decode_kernel,
        grid_spec=grid_spec,
        out_shape=jax.ShapeDtypeStruct((B, 1, GROUP_W), F32),
        compiler_params=_cparams(("parallel", "arbitrary")),
        name="attn_decode",
    )(page_table, lam, r3(q), r3(kn), r3(vn), *([ck] * PAGES_PER_STEP), *([cv] * PAGES_PER_STEP))
    return out.reshape(B, GROUP_W)


def _mix_kernel(oh_ref, gate_ref, od_ref, x_ref, gh_ref, gd_ref, wo_ref, gf_ref, wq_ref,
                hp_ref, hn_ref, qh_ref, *, dscale):
    parts = []
    for h in range(HEADS):
        blk = slice(h * HEAD_W, (h + 1) * HEAD_W)
        parts.append(_rms(oh_ref[:, blk], gh_ref[...]) * gate_ref[:, blk])
    for h in range(HEADS):
        blk = slice(h * HEAD_W, (h + 1) * HEAD_W)
        parts.append(_rms(od_ref[:, blk], gd_ref[...]) * dscale)
    y = jnp.concatenate(parts, axis=-1).astype(BF16)
    hp = x_ref[...] + jnp.dot(y, wo_ref[...], preferred_element_type=F32)
    hp_ref[...] = hp
    hn = _rms(hp, gf_ref[...])
    hn_ref[...] = hn
    qh_ref[...] = jnp.dot(hn.astype(BF16), wq_ref[...], preferred_element_type=F32).astype(BF16)


def _mix(oh, gate, od, x, gh, gd, wo_bf, gf, wq_bf, dscale, tm):
    T = x.shape[0]
    row = lambda i: (i, 0)
    fixed = lambda i: (0, 0)
    half = pl.BlockSpec((tm, GROUP_W), row)
    full = pl.BlockSpec((tm, D_MODEL), row)
    cst = lambda a: pl.BlockSpec(a.shape, fixed)
    nq = wq_bf.shape[1]
    return pl.pallas_call(
        functools.partial(_mix_kernel, dscale=dscale),
        grid=(T // tm,),
        in_specs=[half, half, half, full, cst(gh), cst(gd), cst(wo_bf), cst(gf), cst(wq_bf)],
        out_specs=[full, full, pl.BlockSpec((tm, nq), row)],
        out_shape=[jax.ShapeDtypeStruct((T, D_MODEL), F32), jax.ShapeDtypeStruct((T, D_MODEL), F32),
                   jax.ShapeDtypeStruct((T, nq), BF16)],
        compiler_params=_cparams(("parallel",)),
        name="mix",
    )(oh, gate, od, x, gh, gd, wo_bf, gf, wq_bf)


def _staircase():
    K = PEER_TOPK
    return [(a, b) for a in range(K) for b in range(K) if (a + 1) * (b + 1) <= K]


def _topk_kernel(qh_ref, sk_ref, e_ref, g_ref, s_scr, val_scr, idx_scr, cand_scr, cidx_scr, sc_scr, e_scr):
    K = PEER_TOPK
    NK = PEER_NKEYS
    tb = qh_ref.shape[0]
    half_w = PEER_HEADS * NK
    neg = -jnp.inf

    for c in range(2):
        qc = qh_ref[:, c * half_w:(c + 1) * half_w]
        s = lax.dot_general(sk_ref[c], qc, NT, preferred_element_type=F32)
        s_scr[...] = s.reshape(NK, PEER_HEADS, tb)

        def body(a, carry):
            sv = [s_scr[n] for n in range(NK)]
            m = _tree(jnp.maximum, sv)
            idx = _tree(jnp.minimum, [jnp.where(sv[n] == m, float(n), float(NK)) for n in range(NK)])
            for n in range(NK):
                s_scr[n] = jnp.where(idx == float(n), neg, sv[n])
            val_scr[c, a] = m
            idx_scr[c, a] = idx.astype(I32)
            return carry

        lax.fori_loop(0, K, body, 0)

    pairs = _staircase()
    for i, (a, b) in enumerate(pairs):
        cand_scr[i] = val_scr[0, a] + val_scr[1, b]
        cidx_scr[i] = (idx_scr[0, a] * NK + idx_scr[1, b]) * WORD_ROWS
    flats = [float(a * K + b) for a, b in pairs]
    big = float(K * K)

    def body2(r, carry):
        cs = [cand_scr[i] for i in range(len(pairs))]
        m = _tree(jnp.maximum, cs)
        pos = _tree(jnp.minimum, [jnp.where(cv == m, fl, big) for cv, fl in zip(cs, flats)])
        picks = []
        for i, (cv, fl) in enumerate(zip(cs, flats)):
            hit = pos == fl
            cand_scr[i] = jnp.where(hit, neg, cv)
            picks.append(jnp.where(hit, cidx_scr[i], 0))
        sc_scr[r] = m
        e_scr[r] = _tree(jnp.maximum, picks)
        return carry

    lax.fori_loop(0, K, body2, 0)
    sc = sc_scr[...]
    ex = jnp.exp(sc - sc[0:1])
    g = ex / jnp.sum(ex, axis=0, keepdims=True)
    g_ref[...] = g.reshape(PEER_SEL, tb).T
    e_ref[...] = e_scr[...].reshape(PEER_SEL, tb).T


def _topk(qh, sk_big, tb):
    T = qh.shape[0]
    ncand = len(_staircase())
    hw = (PEER_HEADS, tb)
    return pl.pallas_call(
        _topk_kernel,
        grid=(T // tb,),
        in_specs=[pl.BlockSpec((tb, qh.shape[1]), lambda i: (i, 0)),
                  pl.BlockSpec(sk_big.shape, lambda i: (0, 0, 0))],
        out_specs=[pl.BlockSpec((tb, PEER_SEL), lambda i: (i, 0))] * 2,
        out_shape=[jax.ShapeDtypeStruct((T, PEER_SEL), I32),
                   jax.ShapeDtypeStruct((T, PEER_SEL), F32)],
        scratch_shapes=[pltpu.VMEM((PEER_NKEYS,) + hw, F32),
                        pltpu.VMEM((2, PEER_TOPK) + hw, F32),
                        pltpu.VMEM((2, PEER_TOPK) + hw, I32),
                        pltpu.VMEM((ncand,) + hw, F32),
                        pltpu.VMEM((ncand,) + hw, I32),
                        pltpu.VMEM((PEER_TOPK,) + hw, F32),
                        pltpu.VMEM((PEER_TOPK,) + hw, I32)],
        compiler_params=_cparams(("parallel",)),
        name="peer_topk",
    )(qh, sk_big)


def _pack_kernel(t_ref, o_ref):
    n = t_ref.shape[0]

    def bits(x):
        return pltpu.bitcast(x.astype(BF16).astype(F32), jnp.uint32) >> 16

    for s in range(WORD_ROWS):
        lo = bits(t_ref[:, (2 * s) * HEAD_W:(2 * s + 1) * HEAD_W])
        hi = bits(t_ref[:, (2 * s + 1) * HEAD_W:(2 * s + 2) * HEAD_W])
        o_ref[pl.ds(s, n, stride=WORD_ROWS), :] = pltpu.bitcast((hi << 16) | lo, I32)


def _pack_table(tab, rows=256):
    n = tab.shape[0]
    return pl.pallas_call(
        _pack_kernel,
        grid=(n // rows,),
        in_specs=[pl.BlockSpec((rows, D_MODEL), lambda i: (i, 0))],
        out_specs=pl.BlockSpec((rows * WORD_ROWS, HEAD_W), lambda i: (i, 0)),
        out_shape=jax.ShapeDtypeStruct((n * WORD_ROWS, HEAD_W), I32),
        compiler_params=_cparams(("parallel",)),
        name="pack_table",
    )(tab)


IDX_GROUP = 32
IDX_SLOTS = 2
STG_BUFS = 2


def _stream_index_groups(idx_hbm, idx_smem, sem, tb, n_groups, group_fn):
    step = pl.program_id(0)
    grp = idx_smem.shape[1]
    gps = tb // grp
    slots = min(IDX_SLOTS, gps)
    assert gps % slots == 0

    def copy(group, slot):
        return pltpu.make_async_copy(idx_hbm.at[pl.ds(group * grp, grp)],
                                     idx_smem.at[slot], sem.at[slot])

    @pl.when(step == 0)
    def _():
        for s in range(slots):
            copy(s, s).start()

    def body(it, carry):
        for s in range(slots):
            gl = it * slots + s
            g = step * gps + gl
            copy(g, s).wait()
            group_fn(gl, idx_smem.at[s])

            @pl.when(g + slots < n_groups)
            def _():
                copy(g + slots, s).start()
        return carry

    lax.fori_loop(0, gps // slots, body, 0)


def _gather_rows(idx8, i, tab_ref, stg):
    for j in range(PEER_SEL):
        r = pl.multiple_of(idx8[i, j], WORD_ROWS)
        stg[j * WORD_ROWS:(j + 1) * WORD_ROWS, :] = tab_ref[pl.ds(r, WORD_ROWS), :]


def _peer_u_kernel(idx_hbm, x_ref, g_ref, tab_ref, dmask_ref, gsum_ref, expand_ref, w_ref,
                   stg, c_scr, idx_smem, sem, *, n_groups):
    tb = x_ref.shape[0]
    grp = idx_smem.shape[1]
    dmask = dmask_ref[...]

    def group(gl, idx8):
        for sub in range(grp // 8):
            rows = []
            for i in range(sub * 8, sub * 8 + 8):
                buf = stg.at[i % STG_BUFS]
                _gather_rows(idx8, i, tab_ref, buf)
                ub = pltpu.bitcast(buf[...], BF16)
                xt = x_ref[gl * grp +i].astype(BF16)
                r = lax.dot_general(xt, ub, NT, preferred_element_type=F32)
                rows.append(jnp.sum(r * dmask, axis=0, keepdims=True))
            row0 = pl.multiple_of(gl * grp +sub * 8, 8)
            c_scr[pl.ds(row0, 8), :] = jnp.concatenate(rows, axis=0)

    _stream_index_groups(idx_hbm, idx_smem, sem, tb, n_groups, group)
    c = c_scr[...]
    c_hi = c.astype(BF16)
    c_lo = (c - c_hi.astype(F32)).astype(BF16)
    a = (jnp.dot(c_hi, gsum_ref[...], preferred_element_type=F32)
         + jnp.dot(c_lo, gsum_ref[...], preferred_element_type=F32))
    w = g_ref[...] * jax.nn.gelu(a)
    w_ref[...] = jnp.dot(w.astype(BF16), expand_ref[...], preferred_element_type=F32)


def _peer_v_kernel(idx_hbm, w_ref, tab_ref, dmask_ref, o_ref, stg, idx_smem, sem, *, n_groups):
    tb = w_ref.shape[0]
    grp = idx_smem.shape[1]
    dmask = dmask_ref[...]

    def group(gl, idx8):
        for sub in range(grp // 8):
            w8 = w_ref[pl.ds(pl.multiple_of(gl * grp +sub * 8, 8), 8), :]
            for k in range(8):
                i = sub * 8 + k
                buf = stg.at[i % STG_BUFS]
                _gather_rows(idx8, i, tab_ref, buf)
                vb = pltpu.bitcast(buf[...], BF16)
                lhs = (jnp.broadcast_to(w8[k:k + 1, :], dmask.shape) * dmask).astype(BF16)
                o_ref[gl * grp +i] = jnp.dot(lhs, vb, preferred_element_type=F32)

    _stream_index_groups(idx_hbm, idx_smem, sem, tb, n_groups, group)


def _peer_consts():
    sel_w = PEER_SEL * ROW_CHUNKS
    lane = np.arange(sel_w)
    dmask = (lane[None, :] % ROW_CHUNKS == np.arange(ROW_CHUNKS)[:, None]).astype(np.float32)
    gsum = (lane[:, None] // ROW_CHUNKS == np.arange(PEER_SEL)[None, :]).astype(np.float32)
    return jnp.asarray(dmask, F32), jnp.asarray(gsum, BF16), jnp.asarray(gsum.T, BF16)


def _peer_u(idx, hn, g, tab_u, tb):
    T = hn.shape[0]
    dmask, gsum, expand = _peer_consts()
    sel_w = PEER_SEL * ROW_CHUNKS
    cst = lambda a: pl.BlockSpec(a.shape, lambda i: (0, 0))
    return pl.pallas_call(
        functools.partial(_peer_u_kernel, n_groups=T // min(IDX_GROUP, tb)),
        grid=(T // tb,),
        in_specs=[pl.BlockSpec(memory_space=pl.ANY),
                  pl.BlockSpec((tb, ROW_CHUNKS, HEAD_W), lambda i: (i, 0, 0)),
                  pl.BlockSpec((tb, PEER_SEL), lambda i: (i, 0)),
                  pl.BlockSpec(memory_space=pltpu.VMEM),
                  cst(dmask), cst(gsum), cst(expand)],
        out_specs=pl.BlockSpec((tb, sel_w), lambda i: (i, 0)),
        out_shape=jax.ShapeDtypeStruct((T, sel_w), F32),
        scratch_shapes=[pltpu.VMEM((STG_BUFS, PEER_SEL * WORD_ROWS, HEAD_W), I32),
                        pltpu.VMEM((tb, sel_w), F32),
                        pltpu.SMEM((IDX_SLOTS, min(IDX_GROUP, tb), PEER_SEL), I32),
                        pltpu.SemaphoreType.DMA((IDX_SLOTS,))],
        compiler_params=_cparams(("arbitrary",)),
        name="peer_u",
    )(idx, hn.reshape(T, ROW_CHUNKS, HEAD_W), g, tab_u, dmask, gsum, expand)


def _peer_v(idx, wexp, tab_v, tb):
    T = wexp.shape[0]
    dmask, _, _ = _peer_consts()
    out = pl.pallas_call(
        functools.partial(_peer_v_kernel, n_groups=T // min(IDX_GROUP, tb)),
        grid=(T // tb,),
        in_specs=[pl.BlockSpec(memory_space=pl.ANY),
                  pl.BlockSpec((tb, wexp.shape[1]), lambda i: (i, 0)),
                  pl.BlockSpec(memory_space=pltpu.VMEM),
                  pl.BlockSpec(dmask.shape, lambda i: (0, 0))],
        out_specs=pl.BlockSpec((tb, ROW_CHUNKS, HEAD_W), lambda i: (i, 0, 0)),
        out_shape=jax.ShapeDtypeStruct((T, ROW_CHUNKS, HEAD_W), F32),
        scratch_shapes=[pltpu.VMEM((STG_BUFS, PEER_SEL * WORD_ROWS, HEAD_W), I32),
                        pltpu.SMEM((IDX_SLOTS, min(IDX_GROUP, tb), PEER_SEL), I32),
                        pltpu.SemaphoreType.DMA((IDX_SLOTS,))],
        compiler_params=_cparams(("arbitrary",)),
        name="peer_v",
    )(idx, wexp, tab_v, dmask)
    return out.reshape(T, D_MODEL)


def _ple_kernel(hp_ref, pe_ref, p_ref, wg_ref, bg_ref, wp_ref, gf_ref, y_ref):
    h = hp_ref[...] + pe_ref[...]
    gate = jax.nn.sigmoid(jnp.dot(h.astype(BF16), wg_ref[...], preferred_element_type=F32) + bg_ref[...])
    h = h + gate * jnp.dot(p_ref[...].astype(BF16), wp_ref[...], preferred_element_type=F32)
    y_ref[...] = _rms(h, gf_ref[...])


def _ple(hp, pe, p, wg_bf, bg, wp_bf, gf, tm):
    T = hp.shape[0]
    row = lambda i: (i, 0)
    full = pl.BlockSpec((tm, D_MODEL), row)
    cst = lambda a: pl.BlockSpec(a.shape, lambda i: (0, 0))
    return pl.pallas_call(
        _ple_kernel,
        grid=(T // tm,),
        in_specs=[full, full, pl.BlockSpec((tm, PLE_DIM), row), cst(wg_bf), cst(bg), cst(wp_bf), cst(gf)],
        out_specs=full,
        out_shape=jax.ShapeDtypeStruct((T, D_MODEL), F32),
        compiler_params=_cparams(("parallel",)),
        name="ple",
    )(hp, pe, p, wg_bf, bg, wp_bf, gf)


def _rope_tables(pos):
    half = DA_DH // 2
    freqs = ROPE_THETA ** (-jnp.arange(half, dtype=F32) / half)
    ang = pos.astype(F32)[:, None] * freqs[None, :]
    cos = jnp.tile(jnp.cos(ang), (1, GROUP_W // half))
    sign = jnp.where((jnp.arange(GROUP_W) % DA_DH) < half, -1.0, 1.0).astype(F32)
    sin = jnp.tile(jnp.sin(ang), (1, GROUP_W // half)) * sign[None, :]
    return cos, sin


def _ffn(hp_parts, p, weights, tb_peer):
    hp, hn, qh = hp_parts
    (sk_big, tab_u, tab_v, wg_bf, bg, wp_bf, gfinal) = weights
    T = hp.shape[0]
    tk = 128
    tpad = -(-T // tk) * tk
    qh_p = jnp.pad(qh, ((0, tpad - T), (0, 0))) if tpad != T else qh
    idx, gw = _topk(qh_p, sk_big, tk)
    idx, gw = idx[:T], gw[:T]
    wexp = _peer_u(idx, hn, gw, tab_u, tb_peer)
    pe = _peer_v(idx, wexp, tab_v, tb_peer)
    tm = min(256, T)
    return _ple(hp, pe, p, wg_bf, bg, wp_bf, gfinal, tm)


def kernel(x_prompt, x_sample, p_prompt, p_sample, cache_k, cache_v, state_hgrn, page_table, g_attn, w_in, hgrn_gamma, g_hgrn_norm, lambda_q1, lambda_k1, lambda_q2, lambda_k2, g_diff_norm, w_out, g_ffn, peer_w_query, peer_sub_keys, peer_u, peer_v, ple_w_gate, ple_b_gate, ple_w_proj, g_final):
    Bp, Lp, D = x_prompt.shape
    Bs, Ls, _ = x_sample.shape
    assert D == D_MODEL and Ls == 1 and w_in.shape[0] == 1
    l = 0
    past_len = page_table.shape[1] * cache_k.shape[2]
    lam_init = 0.8 - 0.6 * math.exp(-0.3 * l)
    lam = (jnp.exp(jnp.sum(lambda_q1[l] * lambda_k1[l])) - jnp.exp(jnp.sum(lambda_q2[l] * lambda_k2[l]))
           + lam_init).reshape(1).astype(F32)
    lb = jnp.cumsum(jax.nn.softmax(hgrn_gamma.astype(F32), axis=0), axis=0)[l].reshape(1, GROUP_W)

    w_in_bf = w_in[l].astype(BF16)
    w_out_bf = w_out[l].astype(BF16)
    nqc = PEER_HEADS * 2 * PEER_NKEYS
    wq_bf = (peer_w_query[l].reshape(D, PEER_HEADS, 2, PEER_NKEYS).transpose(0, 2, 1, 3)
             .reshape(D, nqc).astype(BF16))
    sk_big = jnp.einsum('hcnk,hg->cnhgk', peer_sub_keys[l], jnp.eye(PEER_HEADS, dtype=F32)).reshape(
        2, PEER_NKEYS * PEER_HEADS, PEER_HEADS * PEER_NKEYS).astype(BF16)
    tab_u = _pack_table(peer_u[l])
    tab_v = _pack_table(peer_v[l])
    wg_bf = ple_w_gate[l].astype(BF16)
    wp_bf = ple_w_proj[l].astype(BF16)
    row = lambda a: a.reshape(1, -1).astype(F32)
    ffn_w = (sk_big, tab_u, tab_v, wg_bf, row(ple_b_gate[l]), wp_bf, row(g_final))

    def group(x, pos_tab, tm):
        cosf, sins = pos_tab
        return _proj(x, row(g_attn[l]), w_in_bf, lb, cosf, sins, tm)

    def mix(oh, gate, od, x, tm):
        return _mix(oh, gate, od, x, row(g_hgrn_norm[l]), row(g_diff_norm[l]), w_out_bf, row(g_ffn[l]),
                    wq_bf, 1.0 - lam_init, tm)

    Tp = Bp * Lp
    xp = x_prompt.reshape(Tp, D)
    tm_p = math.gcd(Lp, 512)
    hq, hk, lf, hi, gate, k_p, v_p, qb, kb, vb = group(xp, _rope_tables(jnp.arange(Lp)), tm_p)
    o_h, st_p = _hgrn_prompt(hq, hk, hi, lf, Bp, Lp, math.gcd(Lp, 512))
    o_d = _attn_prompt(lam, qb, kb, vb, Bp, Lp, math.gcd(Lp, 512))
    y_p = _ffn(mix(o_h, gate, o_d, xp, min(256, Tp)), p_prompt.reshape(Tp, PLE_DIM), ffn_w, min(64, Tp))

    xs = x_sample.reshape(Bs, D)
    pos_s = jnp.full((Bs,), past_len, dtype=jnp.int32)
    hq, hk, lf, hi, gate, k_s, v_s, qb, kb, vb = group(xs, _rope_tables(pos_s), Bs)
    o_h, st_s = _hgrn_step(hq, hk, hi, lf, state_hgrn.reshape(Bs, HEADS, HEAD_W, HEAD_W))
    npool = cache_k.shape[1]
    o_d = _attn_decode(page_table, lam, qb.astype(F32), k_s.reshape(Bs, GROUP_W), v_s.reshape(Bs, GROUP_W),
                       cache_k.reshape(npool, PAGE * HEADS, HEAD_W),
                       cache_v.reshape(npool, PAGE * HEADS, HEAD_W))
    y_s = _ffn(mix(o_h, gate, o_d, xs, Bs), p_sample.reshape(Bs, PLE_DIM), ffn_w, Bs)

    hd = (HEADS, HEAD_W)
    return (y_p.reshape(Bp, Lp, D), y_s.reshape(Bs, Ls, D),
            k_p.reshape((1, Bp, Lp) + hd), v_p.reshape((1, Bp, Lp) + hd),
            jnp.swapaxes(st_p, -1, -2)[None],
            k_s.reshape((1, Bs, Ls) + hd), v_s.reshape((1, Bs, Ls) + hd), st_s[None])
```

```python
import functools
import math

import numpy as np
import jax
import jax.numpy as jnp
from jax import lax
from jax.experimental import pallas as pl
from jax.experimental.pallas import tpu as pltpu

F32 = jnp.float32
BF16 = jnp.bfloat16
I32 = jnp.int32

D_MODEL = 1024
HEADS = 4
HEAD_W = 128
GROUP_W = HEADS * HEAD_W
DA_DH = 64
ROPE_THETA = 10000.0
HG_CHUNK = 64
PEER_HEADS = 8
PEER_NKEYS = 128
PEER_TOPK = 16
PEER_SEL = PEER_HEADS * PEER_TOPK
PLE_DIM = 256
EPS = 1e-6
PAGE = 128
PAGES_PER_STEP = 8
ROW_CHUNKS = D_MODEL // HEAD_W
WORD_ROWS = ROW_CHUNKS // 2
VMEM_LIMIT = 56 * 1024 * 1024

NT = (((1,), (1,)), ((), ()))
TN = (((0,), (0,)), ((), ()))


def _cparams(sem):
    return pltpu.CompilerParams(dimension_semantics=sem, vmem_limit_bytes=VMEM_LIMIT)


def _tree(op, xs):
    xs = list(xs)
    while len(xs) > 1:
        nxt = [op(xs[i], xs[i + 1]) for i in range(0, len(xs) - 1, 2)]
        if len(xs) % 2:
            nxt.append(xs[-1])
        xs = nxt
    return xs[0]


def _row_reduce(op, lane_reduce, x):
    blocks = [x[:, i * HEAD_W:(i + 1) * HEAD_W] for i in range(x.shape[1] // HEAD_W)]
    return lane_reduce(_tree(op, blocks), axis=-1, keepdims=True)


def _rms(x, g):
    return x * lax.rsqrt(jnp.mean(x * x, axis=-1, keepdims=True) + EPS) * g


def _store_head_rows(ref, x):
    n = x.shape[0]
    for h in range(HEADS):
        ref[pl.ds(h, n, stride=HEADS), :] = x[:, h * HEAD_W:(h + 1) * HEAD_W]


def _proj_kernel(x_ref, g_ref, w_ref, lb_ref, cos_ref, sin_ref,
                 hq_ref, hk_ref, lf_ref, hi_ref, gate_ref, k_ref, v_ref,
                 qb_ref, kb_ref, vb_ref, *, v_transposed):
    xb = _rms(x_ref[...], g_ref[...]).astype(BF16)

    def col(i):
        return jnp.dot(xb, w_ref[:, i * GROUP_W:(i + 1) * GROUP_W], preferred_element_type=F32)

    hq_ref[...] = col(0)
    lb = lb_ref[...]
    f = lb + (1.0 - lb) * jax.nn.sigmoid(col(1))
    lf_ref[...] = jnp.log(f)
    hk_ref[...] = 1.0 - f
    hi_ref[...] = col(2)
    gate_ref[...] = jax.nn.silu(col(3))

    cosf = cos_ref[...]
    sins = sin_ref[...]
    lane = lax.broadcasted_iota(I32, cosf.shape, 1)
    first_half = (lane % DA_DH) < (DA_DH // 2)

    def rope(x):
        swapped = jnp.where(first_half,
                            pltpu.roll(x, GROUP_W - DA_DH // 2, 1),
                            pltpu.roll(x, DA_DH // 2, 1))
        return x * cosf + swapped * sins

    q = rope(col(4))
    qb_ref[...] = (q * (DA_DH ** -0.5)).astype(BF16)
    k = rope(col(5))
    _store_head_rows(k_ref, k)
    kb_ref[...] = k.astype(BF16)
    v = col(6)
    _store_head_rows(v_ref, v)
    vb_ref[...] = (v.T if v_transposed else v).astype(BF16)


def _proj(x, g, w_bf, lb, cosf, sins, tm, v_transposed):
    T = x.shape[0]
    nl = cosf.shape[0] // tm
    row = lambda i: (i, 0)
    fixed = lambda i: (0, 0)
    tab = lambda i: (i % nl, 0)
    f32o = jax.ShapeDtypeStruct((T, GROUP_W), F32)
    bfo = jax.ShapeDtypeStruct((T, GROUP_W), BF16)
    ospec = pl.BlockSpec((tm, GROUP_W), row)
    vspec, vshape = ospec, bfo
    if v_transposed:
        vspec = pl.BlockSpec((GROUP_W, tm), lambda i: (0, i))
        vshape = jax.ShapeDtypeStruct((GROUP_W, T), BF16)
    return pl.pallas_call(
        functools.partial(_proj_kernel, v_transposed=v_transposed),
        grid=(T // tm,),
        in_specs=[pl.BlockSpec((tm, D_MODEL), row),
                  pl.BlockSpec((1, D_MODEL), fixed),
                  pl.BlockSpec(w_bf.shape, fixed),
                  pl.BlockSpec((1, GROUP_W), fixed),
                  pl.BlockSpec((tm, GROUP_W), tab),
                  pl.BlockSpec((tm, GROUP_W), tab)],
        out_specs=[ospec] * 5 + [pl.BlockSpec((tm * HEADS, HEAD_W), row)] * 2 + [ospec] * 2 + [vspec],
        out_shape=[f32o] * 5 + [jax.ShapeDtypeStruct((T * HEADS, HEAD_W), F32)] * 2 + [bfo] * 2 + [vshape],
        compiler_params=_cparams(("parallel",)),
        name="proj",
    )(x, g, w_bf, lb, cosf, sins)


def _hgrn_consts():
    C = HG_CHUNK
    t = np.arange(C)[:, None]
    u = np.arange(C)[None, :]
    mats = [(u <= t)]
    lows, pms = [], []
    h = C // 2
    while h >= 1:
        base = (t // (2 * h)) * (2 * h)
        r = base + h - 1
        lower = t >= base + h
        m = np.where(lower, (u > r) & (u <= t), (u > t) & (u <= r))
        mats.append(m)
        lows.append(np.broadcast_to(lower, (C, HEAD_W)))
        s = np.arange(C)[None, :]
        pms.append((t // (2 * h)) == (s // (2 * h)))
        h //= 2
    mats.append(u > t)
    wall = np.concatenate(mats, axis=0).astype(np.float32)
    return (jnp.asarray(wall, BF16), jnp.asarray(np.stack(lows), F32), jnp.asarray(np.stack(pms), F32))


def _split3(x):
    a = x.astype(BF16)
    r = x - a.astype(F32)
    b = r.astype(BF16)
    c = (r - b.astype(F32)).astype(BF16)
    return a, b, c


HGRN_HEADS_PER_STEP = 2


def _hgrn_kernel(q_ref, k_ref, v_ref, lf_ref, wall_ref, low_ref, pm_ref, o_ref, st_ref, st_scr, *, nlev):
    i = pl.program_id(2)
    C = HG_CHUNK
    nh = st_scr.shape[0]

    @pl.when(i == 0)
    def _():
        st_scr[...] = jnp.zeros_like(st_scr)

    wall = wall_ref[...]
    eye = (lax.broadcasted_iota(I32, (C, C), 0) == lax.broadcasted_iota(I32, (C, C), 1)).astype(F32)

    def chunk(q, k, v, lf, st):
        lcat = jnp.concatenate(_split3(lf), axis=-1)
        d3 = jnp.dot(wall, lcat, preferred_element_type=F32)
        e_all = jnp.exp(d3[:, :HEAD_W] + d3[:, HEAD_W:2 * HEAD_W] + d3[:, 2 * HEAD_W:])
        e_g = e_all[0:C]
        e_last = e_all[C - 1:C]
        e_k = e_all[(nlev + 1) * C:(nlev + 2) * C]
        vb = v.astype(BF16)
        o = lax.dot_general((q * e_g).astype(BF16), st.astype(BF16), NT, preferred_element_type=F32)
        a = eye * jnp.sum(q * k, axis=-1, keepdims=True)
        for l in range(nlev):
            e_l = e_all[(l + 1) * C:(l + 2) * C]
            low = low_ref[l]
            ql = (q * e_l * low).astype(BF16)
            kl = (k * e_l * (1.0 - low)).astype(BF16)
            a = a + lax.dot_general(ql, kl, NT, preferred_element_type=F32) * pm_ref[l]
        o = o + jnp.dot(a.astype(BF16), vb, preferred_element_type=F32)
        kd = (k * e_k).astype(BF16)
        return o, e_last * st + lax.dot_general(vb, kd, TN, preferred_element_type=F32)

    sts = [st_scr[hh] for hh in range(nh)]
    for c in range(q_ref.shape[0] // C):
        sl = slice(c * C, (c + 1) * C)
        for hh in range(nh):
            cols = slice(hh * HEAD_W, (hh + 1) * HEAD_W)
            o, sts[hh] = chunk(q_ref[sl, cols], k_ref[sl, cols], v_ref[sl, cols], lf_ref[sl, cols], sts[hh])
            o_ref[sl, cols] = o
    for hh in range(nh):
        st_scr[hh] = sts[hh]

    @pl.when(i == pl.num_programs(2) - 1)
    def _():
        for hh in range(nh):
            st_ref[0, hh] = sts[hh]


def _hgrn_prompt(hq, hk, hi, lf, B, L, lb_rows):
    wall, low, pm = _hgrn_consts()
    nlev = low.shape[0]
    nblk = L // lb_rows
    nh = HGRN_HEADS_PER_STEP
    blk = pl.BlockSpec((lb_rows, nh * HEAD_W), lambda b, h, i: (b * nblk + i, h))
    cst = lambda a: pl.BlockSpec(a.shape, lambda b, h, i: (0,) * a.ndim)
    return pl.pallas_call(
        functools.partial(_hgrn_kernel, nlev=nlev),
        grid=(B, HEADS // nh, nblk),
        in_specs=[blk, blk, blk, blk, cst(wall), cst(low), cst(pm)],
        out_specs=[blk, pl.BlockSpec((1, nh, HEAD_W, HEAD_W), lambda b, h, i: (b, h, 0, 0))],
        out_shape=[jax.ShapeDtypeStruct(hq.shape, F32),
                   jax.ShapeDtypeStruct((B, HEADS, HEAD_W, HEAD_W), F32)],
        scratch_shapes=[pltpu.VMEM((nh, HEAD_W, HEAD_W), F32)],
        compiler_params=_cparams(("parallel", "parallel", "arbitrary")),
        name="hgrn_prompt",
    )(hq, hk, hi, lf, wall, low, pm)


def _hgrn_step_kernel(qc_ref, kc_ref, lfc_ref, v_ref, s0_ref, o_ref, s_ref):
    qc, kc = qc_ref[0, 0], kc_ref[0, 0]
    dec = jnp.exp(lfc_ref[0, 0])
    v = v_ref[0, 0]
    s0 = s0_ref[0, 0]
    s_ref[0, 0] = dec * s0 + kc * v
    o_ref[0, 0] = (jnp.sum((qc * dec) * s0, axis=0, keepdims=True)
                   + jnp.sum(qc * kc, axis=0, keepdims=True) * v)


def _hgrn_step(hq, hk, hi, lf, s0):
    B = hq.shape[0]
    colv = lambda a: a.reshape(B, HEADS, HEAD_W, 1)
    cspec = pl.BlockSpec((1, 1, HEAD_W, 1), lambda b, h: (b, h, 0, 0))
    rspec = pl.BlockSpec((1, 1, 1, HEAD_W), lambda b, h: (b, h, 0, 0))
    sspec = pl.BlockSpec((1, 1, HEAD_W, HEAD_W), lambda b, h: (b, h, 0, 0))
    o, s = pl.pallas_call(
        _hgrn_step_kernel,
        grid=(B, HEADS),
        in_specs=[cspec, cspec, cspec, rspec, sspec],
        out_specs=[rspec, sspec],
        out_shape=[jax.ShapeDtypeStruct((B, HEADS, 1, HEAD_W), F32),
                   jax.ShapeDtypeStruct((B, HEADS, HEAD_W, HEAD_W), F32)],
        compiler_params=_cparams(("parallel", "parallel")),
        name="hgrn_step",
    )(colv(hq), colv(hk), colv(lf), hi.reshape(B, HEADS, 1, HEAD_W), s0)
    return o.reshape(B, GROUP_W), s


ATTN_QW = 128


def _attn_kernel(lam_ref, q_ref, k_ref, vt_ref, o_ref, qs_scr, m_scr, l_scr, acc_scr):
    qi = pl.program_id(1)
    ki = pl.program_id(2)
    bq = q_ref.shape[0]
    bk = k_ref.shape[0]
    nsub = 2 * bq // ATTN_QW

    @pl.when(ki == 0)
    def _():
        lane = lax.broadcasted_iota(I32, (bq, HEAD_W), 1)
        for h in range(HEADS):
            q = q_ref[:, h * HEAD_W:(h + 1) * HEAD_W]
            zero = jnp.zeros_like(q)
            qs_scr[h] = jnp.concatenate([jnp.where(lane < DA_DH, q, zero),
                                         jnp.where(lane >= DA_DH, q, zero)], axis=0)
        m_scr[...] = jnp.full(m_scr.shape, -jnp.inf, F32)
        l_scr[...] = jnp.zeros_like(l_scr)
        acc_scr[...] = jnp.zeros_like(acc_scr)

    @pl.when(ki <= qi)
    def _():
        kpos = lax.broadcasted_iota(I32, (bk, ATTN_QW), 0)
        qpos = lax.broadcasted_iota(I32, (bk, ATTN_QW), 1)
        off_diag = ki < qi
        for h in range(HEADS):
            kh = k_ref[:, h * HEAD_W:(h + 1) * HEAD_W]
            vth = vt_ref[h * HEAD_W:(h + 1) * HEAD_W, :]
            for sb in range(nsub):
                cols = slice(sb * ATTN_QW, (sb + 1) * ATTN_QW)
                st = lax.dot_general(kh, qs_scr[h, cols, :], NT, preferred_element_type=F32)
                keep = (kpos <= qpos + (sb * ATTN_QW) % bq) | off_diag
                st = jnp.where(keep, st, -jnp.inf)
                m_old = m_scr[h, :, cols]
                smax = _tree(jnp.maximum, [st[r * 8:(r + 1) * 8] for r in range(bk // 8)])
                m_new = jnp.maximum(m_old, jnp.max(smax, axis=0, keepdims=True))
                alpha = jnp.exp(m_old - m_new)
                pt = jnp.exp(st - m_new)
                psum = _tree(jnp.add, [pt[r * 8:(r + 1) * 8] for r in range(bk // 8)])
                l_scr[h, :, cols] = alpha * l_scr[h, :, cols] + jnp.sum(psum, axis=0, keepdims=True)
                acc_scr[h, :, cols] = (alpha * acc_scr[h, :, cols]
                                       + jnp.dot(vth, pt.astype(BF16), preferred_element_type=F32))
                m_scr[h, :, cols] = m_new

    @pl.when(ki == qi)
    def _():
        lam = lam_ref[0]
        for h in range(HEADS):
            r = acc_scr[h] / l_scr[h]
            o_ref[:, h * HEAD_W:(h + 1) * HEAD_W] = (r[:, :bq] - lam * r[:, bq:]).T


def _attn_prompt(lam, qb, kb, vbt, B, L, bq):
    nq = L // bq
    qspec = pl.BlockSpec((bq, GROUP_W), lambda b, qi, ki: (b * nq + qi, 0))
    kspec = pl.BlockSpec((bq, GROUP_W), lambda b, qi, ki: (b * nq + jnp.minimum(ki, qi), 0))
    vspec = pl.BlockSpec((GROUP_W, bq), lambda b, qi, ki: (0, b * nq + jnp.minimum(ki, qi)))
    return pl.pallas_call(
        _attn_kernel,
        grid=(B, nq, nq),
        in_specs=[pl.BlockSpec(memory_space=pltpu.SMEM), qspec, kspec, vspec],
        out_specs=qspec,
        out_shape=jax.ShapeDtypeStruct(qb.shape, F32),
        scratch_shapes=[pltpu.VMEM((HEADS, 2 * bq, HEAD_W), BF16),
                        pltpu.VMEM((HEADS, 1, 2 * bq), F32),
                        pltpu.VMEM((HEADS, 1, 2 * bq), F32),
                        pltpu.VMEM((HEADS, HEAD_W, 2 * bq), F32)],
        compiler_params=_cparams(("parallel", "parallel", "arbitrary")),
        name="attn_prompt",
    )(lam, qb, kb, vbt)


def _decode_kernel(pt_ref, lam_ref, q_ref, kn_ref, vn_ref, *rest):
    npg = PAGES_PER_STEP
    k_refs, v_refs = rest[:npg], rest[npg:2 * npg]
    o_ref, kcat, vcat, m_scr, l_scr, acc_scr = rest[2 * npg:]
    i = pl.program_id(1)
    nrow = 2 * HEADS

    rowi = lax.broadcasted_iota(I32, (nrow, GROUP_W), 0)
    lane = lax.broadcasted_iota(I32, (nrow, GROUP_W), 1)
    sel = (lane // HEAD_W == rowi // 2) & ((lane % HEAD_W) // DA_DH == rowi % 2)
    qbd = jnp.where(sel, jnp.broadcast_to(q_ref[0], (nrow, GROUP_W)), 0.0)

    @pl.when(i == 0)
    def _():
        m_scr[...] = jnp.full(m_scr.shape, -jnp.inf, F32)
        l_scr[...] = jnp.zeros_like(l_scr)
        acc_scr[...] = jnp.zeros_like(acc_scr)

    for p in range(npg):
        for h in range(HEADS):
            rows = slice(p * PAGE, (p + 1) * PAGE)
            cols = slice(h * HEAD_W, (h + 1) * HEAD_W)
            kcat[rows, cols] = k_refs[p][0, pl.ds(h, PAGE, stride=HEADS), :].astype(BF16)
            vcat[rows, cols] = v_refs[p][0, pl.ds(h, PAGE, stride=HEADS), :].astype(BF16)
    s = lax.dot_general(qbd.astype(BF16), kcat[...], NT, preferred_element_type=F32)
    m_old = m_scr[...]
    m_new = jnp.maximum(m_old, _row_reduce(jnp.maximum, jnp.max, s))
    alpha = jnp.exp(m_old - m_new)
    p_ = jnp.exp(s - m_new)
    l_scr[...] = alpha * l_scr[...] + _row_reduce(jnp.add, jnp.sum, p_)
    acc_scr[...] = alpha * acc_scr[...] + jnp.dot(p_.astype(BF16), vcat[...], preferred_element_type=F32)
    m_scr[...] = m_new

    @pl.when(i == pl.num_programs(1) - 1)
    def _():
        s_new = jnp.sum(qbd * kn_ref[0], axis=-1, keepdims=True)
        m_o = m_scr[...]
        m_n = jnp.maximum(m_o, s_new)
        al = jnp.exp(m_o - m_n)
        pn = jnp.exp(s_new - m_n)
        l_f = al * l_scr[...] + pn
        r = (al * acc_scr[...] + pn * vn_ref[0]) / l_f
        lam = lam_ref[0]
        outs = []
        for h in range(HEADS):
            blk = slice(h * HEAD_W, (h + 1) * HEAD_W)
            outs.append(r[2 * h:2 * h + 1, blk] - lam * r[2 * h + 1:2 * h + 2, blk])
        o_ref[0] = jnp.concatenate(outs, axis=-1)


def _attn_decode(page_table, lam, q, kn, vn, ck, cv):
    B = q.shape[0]
    nsteps = page_table.shape[1] // PAGES_PER_STEP
    tok = pl.BlockSpec((1, 1, GROUP_W), lambda b, i, pt: (b, 0, 0))

    def page(p):
        return pl.BlockSpec((1, PAGE * HEADS, HEAD_W),
                            lambda b, i, pt: (pt[b, i * PAGES_PER_STEP + p], 0, 0))

    pages = [page(p) for p in range(PAGES_PER_STEP)]
    grid_spec = pltpu.PrefetchScalarGridSpec(
        num_scalar_prefetch=1,
        grid=(B, nsteps),
        in_specs=[pl.BlockSpec(memory_space=pltpu.SMEM), tok, tok, tok] + pages + pages,
        out_specs=tok,
        scratch_shapes=[pltpu.VMEM((PAGES_PER_STEP * PAGE, GROUP_W), BF16),
                        pltpu.VMEM((PAGES_PER_STEP * PAGE, GROUP_W), BF16),
                        pltpu.VMEM((2 * HEADS, 1), F32),
                        pltpu.VMEM((2 * HEADS, 1), F32),
                        pltpu.VMEM((2 * HEADS, GROUP_W), F32)])
    r3 = lambda a: a.reshape(B, 1, GROUP_W)
    out = pl.pallas_call(
        _decode_kernel,
        grid_spec=grid_spec,
        out_shape=jax.ShapeDtypeStruct((B, 1, GROUP_W), F32),
        compiler_params=_cparams(("parallel", "arbitrary")),
        name="attn_decode",
    )(page_table, lam, r3(q), r3(kn), r3(vn), *([ck] * PAGES_PER_STEP), *([cv] * PAGES_PER_STEP))
    return out.reshape(B, GROUP_W)


def _mix_kernel(oh_ref, gate_ref, od_ref, x_ref, gh_ref, gd_ref, wo_ref, gf_ref, wq_ref,
                hp_ref, hn_ref, qh_ref, *, dscale):
    parts = []
    for h in range(HEADS):
        blk = slice(h * HEAD_W, (h + 1) * HEAD_W)
        parts.append(_rms(oh_ref[:, blk], gh_ref[...]) * gate_ref[:, blk])
    for h in range(HEADS):
        blk = slice(h * HEAD_W, (h + 1) * HEAD_W)
        parts.append(_rms(od_ref[:, blk], gd_ref[...]) * dscale)
    y = jnp.concatenate(parts, axis=-1).astype(BF16)
    hp = x_ref[...] + jnp.dot(y, wo_ref[...], preferred_element_type=F32)
    hp_ref[...] = hp
    hn = _rms(hp, gf_ref[...])
    hn_ref[...] = hn
    qh_ref[...] = jnp.dot(hn.astype(BF16), wq_ref[...], preferred_element_type=F32).astype(BF16)


def _mix(oh, gate, od, x, gh, gd, wo_bf, gf, wq_bf, dscale, tm):
    T = x.shape[0]
    row = lambda i: (i, 0)
    fixed = lambda i: (0, 0)
    half = pl.BlockSpec((tm, GROUP_W), row)
    full = pl.BlockSpec((tm, D_MODEL), row)
    cst = lambda a: pl.BlockSpec(a.shape, fixed)
    nq = wq_bf.shape[1]
    return pl.pallas_call(
        functools.partial(_mix_kernel, dscale=dscale),
        grid=(T // tm,),
        in_specs=[half, half, half, full, cst(gh), cst(gd), cst(wo_bf), cst(gf), cst(wq_bf)],
        out_specs=[full, full, pl.BlockSpec((tm, nq), row)],
        out_shape=[jax.ShapeDtypeStruct((T, D_MODEL), F32), jax.ShapeDtypeStruct((T, D_MODEL), F32),
                   jax.ShapeDtypeStruct((T, nq), BF16)],
        compiler_params=_cparams(("parallel",)),
        name="mix",
    )(oh, gate, od, x, gh, gd, wo_bf, gf, wq_bf)


def _staircase():
    K = PEER_TOPK
    return [(a, b) for a in range(K) for b in range(K) if (a + 1) * (b + 1) <= K]


def _topk_kernel(qh_ref, sk_ref, e_ref, g_ref, s_scr, val_scr, idx_scr, cand_scr, cidx_scr, sc_scr, e_scr):
    K = PEER_TOPK
    NK = PEER_NKEYS
    tb = qh_ref.shape[0]
    half_w = PEER_HEADS * NK
    neg = -jnp.inf

    for c in range(2):
        qc = qh_ref[:, c * half_w:(c + 1) * half_w]
        s = lax.dot_general(sk_ref[c], qc, NT, preferred_element_type=F32)
        s_scr[...] = s.reshape(NK, PEER_HEADS, tb)

        def body(a, carry):
            sv = [s_scr[n] for n in range(NK)]
            m = _tree(jnp.maximum, sv)
            idx = _tree(jnp.minimum, [jnp.where(sv[n] == m, float(n), float(NK)) for n in range(NK)])
            for n in range(NK):
                s_scr[n] = jnp.where(idx == float(n), neg, sv[n])
            val_scr[c, a] = m
            idx_scr[c, a] = idx.astype(I32)
            return carry

        lax.fori_loop(0, K, body, 0)

    pairs = _staircase()
    for i, (a, b) in enumerate(pairs):
        cand_scr[i] = val_scr[0, a] + val_scr[1, b]
        cidx_scr[i] = (idx_scr[0, a] * NK + idx_scr[1, b]) * WORD_ROWS
    flats = [float(a * K + b) for a, b in pairs]
    big = float(K * K)

    def body2(r, carry):
        cs = [cand_scr[i] for i in range(len(pairs))]
        m = _tree(jnp.maximum, cs)
        pos = _tree(jnp.minimum, [jnp.where(cv == m, fl, big) for cv, fl in zip(cs, flats)])
        picks = []
        for i, (cv, fl) in enumerate(zip(cs, flats)):
            hit = pos == fl
            cand_scr[i] = jnp.where(hit, neg, cv)
            picks.append(jnp.where(hit, cidx_scr[i], 0))
        sc_scr[r] = m
        e_scr[r] = _tree(jnp.maximum, picks)
        return carry

    lax.fori_loop(0, K, body2, 0)
    sc = sc_scr[...]
    ex = jnp.exp(sc - sc[0:1])
    g = ex / jnp.sum(ex, axis=0, keepdims=True)
    g_ref[...] = g.reshape(PEER_SEL, tb).T
    e_ref[...] = e_scr[...].reshape(PEER_SEL, tb).T


def _topk(qh, sk_big, tb):
    T = qh.shape[0]
    ncand = len(_staircase())
    hw = (PEER_HEADS, tb)
    return pl.pallas_call(
        _topk_kernel,
        grid=(T // tb,),
        in_specs=[pl.BlockSpec((tb, qh.shape[1]), lambda i: (i, 0)),
                  pl.BlockSpec(sk_big.shape, lambda i: (0, 0, 0))],
        out_specs=[pl.BlockSpec((tb, PEER_SEL), lambda i: (i, 0))] * 2,
        out_shape=[jax.ShapeDtypeStruct((T, PEER_SEL), I32),
                   jax.ShapeDtypeStruct((T, PEER_SEL), F32)],
        scratch_shapes=[pltpu.VMEM((PEER_NKEYS,) + hw, F32),
                        pltpu.VMEM((2, PEER_TOPK) + hw, F32),
                        pltpu.VMEM((2, PEER_TOPK) + hw, I32),
                        pltpu.VMEM((ncand,) + hw, F32),
                        pltpu.VMEM((ncand,) + hw, I32),
                        pltpu.VMEM((PEER_TOPK,) + hw, F32),
                        pltpu.VMEM((PEER_TOPK,) + hw, I32)],
        compiler_params=_cparams(("parallel",)),
        name="peer_topk",
    )(qh, sk_big)


def _pack_kernel(t_ref, o_ref):
    n = t_ref.shape[0]

    def bits(x):
        return pltpu.bitcast(x.astype(BF16).astype(F32), jnp.uint32) >> 16

    for s in range(WORD_ROWS):
        lo = bits(t_ref[:, (2 * s) * HEAD_W:(2 * s + 1) * HEAD_W])
        hi = bits(t_ref[:, (2 * s + 1) * HEAD_W:(2 * s + 2) * HEAD_W])
        o_ref[pl.ds(s, n, stride=WORD_ROWS), :] = pltpu.bitcast((hi << 16) | lo, I32)


def _pack_table(tab, rows=256):
    n = tab.shape[0]
    return pl.pallas_call(
        _pack_kernel,
        grid=(n // rows,),
        in_specs=[pl.BlockSpec((rows, D_MODEL), lambda i: (i, 0))],
        out_specs=pl.BlockSpec((rows * WORD_ROWS, HEAD_W), lambda i: (i, 0)),
        out_shape=jax.ShapeDtypeStruct((n * WORD_ROWS, HEAD_W), I32),
        compiler_params=_cparams(("parallel",)),
        name="pack_table",
    )(tab)


IDX_GROUP = 32
IDX_SLOTS = 2
STG_BUFS = 2


def _stream_index_groups(idx_hbm, idx_smem, sem, tb, n_groups, group_fn):
    step = pl.program_id(0)
    grp = idx_smem.shape[1]
    gps = tb // grp
    slots = min(IDX_SLOTS, gps)
    assert gps % slots == 0

    def copy(group, slot):
        return pltpu.make_async_copy(idx_hbm.at[pl.ds(group * grp, grp)],
                                     idx_smem.at[slot], sem.at[slot])

    @pl.when(step == 0)
    def _():
        for s in range(slots):
            copy(s, s).start()

    def body(it, carry):
        for s in range(slots):
            gl = it * slots + s
            g = step * gps + gl
            copy(g, s).wait()
            group_fn(gl, idx_smem.at[s])

            @pl.when(g + slots < n_groups)
            def _():
                copy(g + slots, s).start()
        return carry

    lax.fori_loop(0, gps // slots, body, 0)


def _gather_rows(idx8, i, tab_ref, stg):
    for j in range(PEER_SEL):
        r = pl.multiple_of(idx8[i, j], WORD_ROWS)
        stg[j * WORD_ROWS:(j + 1) * WORD_ROWS, :] = tab_ref[pl.ds(r, WORD_ROWS), :]


def _peer_u_kernel(idx_hbm, x_ref, g_ref, tab_ref, dmask_ref, gsum_ref, expand_ref, w_ref,
                   stg, c_scr, idx_smem, sem, *, n_groups):
    tb = x_ref.shape[0]
    grp = idx_smem.shape[1]
    dmask = dmask_ref[...]

    def group(gl, idx8):
        for sub in range(grp // 8):
            rows = []
            for i in range(sub * 8, sub * 8 + 8):
                buf = stg.at[i % STG_BUFS]
                _gather_rows(idx8, i, tab_ref, buf)
                ub = pltpu.bitcast(buf[...], BF16)
                xt = x_ref[gl * grp +i].astype(BF16)
                r = lax.dot_general(xt, ub, NT, preferred_element_type=F32)
                rows.append(jnp.sum(r * dmask, axis=0, keepdims=True))
            row0 = pl.multiple_of(gl * grp +sub * 8, 8)
            c_scr[pl.ds(row0, 8), :] = jnp.concatenate(rows, axis=0)

    _stream_index_groups(idx_hbm, idx_smem, sem, tb, n_groups, group)
    c = c_scr[...]
    c_hi = c.astype(BF16)
    c_lo = (c - c_hi.astype(F32)).astype(BF16)
    a = (jnp.dot(c_hi, gsum_ref[...], preferred_element_type=F32)
         + jnp.dot(c_lo, gsum_ref[...], preferred_element_type=F32))
    w = g_ref[...] * jax.nn.gelu(a)
    w_ref[...] = jnp.dot(w.astype(BF16), expand_ref[...], preferred_element_type=F32)


def _peer_v_kernel(idx_hbm, w_ref, tab_ref, dmask_ref, o_ref, stg, idx_smem, sem, *, n_groups):
    tb = w_ref.shape[0]
    grp = idx_smem.shape[1]
    dmask = dmask_ref[...]

    def group(gl, idx8):
        for sub in range(grp // 8):
            w8 = w_ref[pl.ds(pl.multiple_of(gl * grp +sub * 8, 8), 8), :]
            for k in range(8):
                i = sub * 8 + k
                buf = stg.at[i % STG_BUFS]
                _gather_rows(idx8, i, tab_ref, buf)
                vb = pltpu.bitcast(buf[...], BF16)
                lhs = (jnp.broadcast_to(w8[k:k + 1, :], dmask.shape) * dmask).astype(BF16)
                o_ref[gl * grp +i] = jnp.dot(lhs, vb, preferred_element_type=F32)

    _stream_index_groups(idx_hbm, idx_smem, sem, tb, n_groups, group)


def _peer_consts():
    sel_w = PEER_SEL * ROW_CHUNKS
    lane = np.arange(sel_w)
    dmask = (lane[None, :] % ROW_CHUNKS == np.arange(ROW_CHUNKS)[:, None]).astype(np.float32)
    gsum = (lane[:, None] // ROW_CHUNKS == np.arange(PEER_SEL)[None, :]).astype(np.float32)
    return jnp.asarray(dmask, F32), jnp.asarray(gsum, BF16), jnp.asarray(gsum.T, BF16)


def _peer_u(idx, hn, g, tab_u, tb):
    T = hn.shape[0]
    dmask, gsum, expand = _peer_consts()
    sel_w = PEER_SEL * ROW_CHUNKS
    cst = lambda a: pl.BlockSpec(a.shape, lambda i: (0, 0))
    return pl.pallas_call(
        functools.partial(_peer_u_kernel, n_groups=T // min(IDX_GROUP, tb)),
        grid=(T // tb,),
        in_specs=[pl.BlockSpec(memory_space=pl.ANY),
                  pl.BlockSpec((tb, ROW_CHUNKS, HEAD_W), lambda i: (i, 0, 0)),
                  pl.BlockSpec((tb, PEER_SEL), lambda i: (i, 0)),
                  pl.BlockSpec(memory_space=pltpu.VMEM),
                  cst(dmask), cst(gsum), cst(expand)],
        out_specs=pl.BlockSpec((tb, sel_w), lambda i: (i, 0)),
        out_shape=jax.ShapeDtypeStruct((T, sel_w), F32),
        scratch_shapes=[pltpu.VMEM((STG_BUFS, PEER_SEL * WORD_ROWS, HEAD_W), I32),
                        pltpu.VMEM((tb, sel_w), F32),
                        pltpu.SMEM((IDX_SLOTS, min(IDX_GROUP, tb), PEER_SEL), I32),
                        pltpu.SemaphoreType.DMA((IDX_SLOTS,))],
        compiler_params=_cparams(("arbitrary",)),
        name="peer_u",
    )(idx, hn.reshape(T, ROW_CHUNKS, HEAD_W), g, tab_u, dmask, gsum, expand)


def _peer_v(idx, wexp, tab_v, tb):
    T = wexp.shape[0]
    dmask, _, _ = _peer_consts()
    out = pl.pallas_call(
        functools.partial(_peer_v_kernel, n_groups=T // min(IDX_GROUP, tb)),
        grid=(T // tb,),
        in_specs=[pl.BlockSpec(memory_space=pl.ANY),
                  pl.BlockSpec((tb, wexp.shape[1]), lambda i: (i, 0)),
                  pl.BlockSpec(memory_space=pltpu.VMEM),
                  pl.BlockSpec(dmask.shape, lambda i: (0, 0))],
        out_specs=pl.BlockSpec((tb, ROW_CHUNKS, HEAD_W), lambda i: (i, 0, 0)),
        out_shape=jax.ShapeDtypeStruct((T, ROW_CHUNKS, HEAD_W), F32),
        scratch_shapes=[pltpu.VMEM((STG_BUFS, PEER_SEL * WORD_ROWS, HEAD_W), I32),
                        pltpu.SMEM((IDX_SLOTS, min(IDX_GROUP, tb), PEER_SEL), I32),
                        pltpu.SemaphoreType.DMA((IDX_SLOTS,))],
        compiler_params=_cparams(("arbitrary",)),
        name="peer_v",
    )(idx, wexp, tab_v, dmask)
    return out.reshape(T, D_MODEL)


def _ple_kernel(hp_ref, pe_ref, p_ref, wg_ref, bg_ref, wp_ref, gf_ref, y_ref):
    h = hp_ref[...] + pe_ref[...]
    gate = jax.nn.sigmoid(jnp.dot(h.astype(BF16), wg_ref[...], preferred_element_type=F32) + bg_ref[...])
    h = h + gate * jnp.dot(p_ref[...].astype(BF16), wp_ref[...], preferred_element_type=F32)
    y_ref[...] = _rms(h, gf_ref[...])


def _ple(hp, pe, p, wg_bf, bg, wp_bf, gf, tm):
    T = hp.shape[0]
    row = lambda i: (i, 0)
    full = pl.BlockSpec((tm, D_MODEL), row)
    cst = lambda a: pl.BlockSpec(a.shape, lambda i: (0, 0))
    return pl.pallas_call(
        _ple_kernel,
        grid=(T // tm,),
        in_specs=[full, full, pl.BlockSpec((tm, PLE_DIM), row), cst(wg_bf), cst(bg), cst(wp_bf), cst(gf)],
        out_specs=full,
        out_shape=jax.ShapeDtypeStruct((T, D_MODEL), F32),
        compiler_params=_cparams(("parallel",)),
        name="ple",
    )(hp, pe, p, wg_bf, bg, wp_bf, gf)


def _rope_tables(pos):
    half = DA_DH // 2
    freqs = ROPE_THETA ** (-jnp.arange(half, dtype=F32) / half)
    ang = pos.astype(F32)[:, None] * freqs[None, :]
    cos = jnp.tile(jnp.cos(ang), (1, GROUP_W // half))
    sign = jnp.where((jnp.arange(GROUP_W) % DA_DH) < half, -1.0, 1.0).astype(F32)
    sin = jnp.tile(jnp.sin(ang), (1, GROUP_W // half)) * sign[None, :]
    return cos, sin


def _ffn(hp_parts, p, weights, tb_peer):
    hp, hn, qh = hp_parts
    (sk_big, tab_u, tab_v, wg_bf, bg, wp_bf, gfinal) = weights
    T = hp.shape[0]
    tk = 128
    tpad = -(-T // tk) * tk
    qh_p = jnp.pad(qh, ((0, tpad - T), (0, 0))) if tpad != T else qh
    idx, gw = _topk(qh_p, sk_big, tk)
    idx, gw = idx[:T], gw[:T]
    wexp = _peer_u(idx, hn, gw, tab_u, tb_peer)
    pe = _peer_v(idx, wexp, tab_v, tb_peer)
    tm = min(256, T)
    return _ple(hp, pe, p, wg_bf, bg, wp_bf, gfinal, tm)


def kernel(x_prompt, x_sample, p_prompt, p_sample, cache_k, cache_v, state_hgrn, page_table, g_attn, w_in, hgrn_gamma, g_hgrn_norm, lambda_q1, lambda_k1, lambda_q2, lambda_k2, g_diff_norm, w_out, g_ffn, peer_w_query, peer_sub_keys, peer_u, peer_v, ple_w_gate, ple_b_gate, ple_w_proj, g_final):
    Bp, Lp, D = x_prompt.shape
    Bs, Ls, _ = x_sample.shape
    assert D == D_MODEL and Ls == 1 and w_in.shape[0] == 1
    l = 0
    past_len = page_table.shape[1] * cache_k.shape[2]
    lam_init = 0.8 - 0.6 * math.exp(-0.3 * l)
    lam = (jnp.exp(jnp.sum(lambda_q1[l] * lambda_k1[l])) - jnp.exp(jnp.sum(lambda_q2[l] * lambda_k2[l]))
           + lam_init).reshape(1).astype(F32)
    lb = jnp.cumsum(jax.nn.softmax(hgrn_gamma.astype(F32), axis=0), axis=0)[l].reshape(1, GROUP_W)

    w_in_bf = w_in[l].astype(BF16)
    w_out_bf = w_out[l].astype(BF16)
    nqc = PEER_HEADS * 2 * PEER_NKEYS
    wq_bf = (peer_w_query[l].reshape(D, PEER_HEADS, 2, PEER_NKEYS).transpose(0, 2, 1, 3)
             .reshape(D, nqc).astype(BF16))
    sk_big = jnp.einsum('hcnk,hg->cnhgk', peer_sub_keys[l], jnp.eye(PEER_HEADS, dtype=F32)).reshape(
        2, PEER_NKEYS * PEER_HEADS, PEER_HEADS * PEER_NKEYS).astype(BF16)
    tab_u = _pack_table(peer_u[l])
    tab_v = _pack_table(peer_v[l])
    wg_bf = ple_w_gate[l].astype(BF16)
    wp_bf = ple_w_proj[l].astype(BF16)
    row = lambda a: a.reshape(1, -1).astype(F32)
    ffn_w = (sk_big, tab_u, tab_v, wg_bf, row(ple_b_gate[l]), wp_bf, row(g_final))

    def group(x, pos_tab, tm, v_transposed):
        cosf, sins = pos_tab
        return _proj(x, row(g_attn[l]), w_in_bf, lb, cosf, sins, tm, v_transposed)

    def mix(oh, gate, od, x, tm):
        return _mix(oh, gate, od, x, row(g_hgrn_norm[l]), row(g_diff_norm[l]), w_out_bf, row(g_ffn[l]),
                    wq_bf, 1.0 - lam_init, tm)

    Tp = Bp * Lp
    xp = x_prompt.reshape(Tp, D)
    tm_p = math.gcd(Lp, 512)
    hq, hk, lf, hi, gate, k_p, v_p, qb, kb, vb = group(xp, _rope_tables(jnp.arange(Lp)), tm_p, True)
    o_h, st_p = _hgrn_prompt(hq, hk, hi, lf, Bp, Lp, math.gcd(Lp, 512))
    o_d = _attn_prompt(lam, qb, kb, vb, Bp, Lp, math.gcd(Lp, 512))
    y_p = _ffn(mix(o_h, gate, o_d, xp, min(256, Tp)), p_prompt.reshape(Tp, PLE_DIM), ffn_w, min(64, Tp))

    xs = x_sample.reshape(Bs, D)
    pos_s = jnp.full((Bs,), past_len, dtype=jnp.int32)
    hq, hk, lf, hi, gate, k_s, v_s, qb, kb, vb = group(xs, _rope_tables(pos_s), Bs, False)
    o_h, st_s = _hgrn_step(hq, hk, hi, lf, state_hgrn.reshape(Bs, HEADS, HEAD_W, HEAD_W))
    npool = cache_k.shape[1]
    o_d = _attn_decode(page_table, lam, qb.astype(F32), k_s.reshape(Bs, GROUP_W), v_s.reshape(Bs, GROUP_W),
                       cache_k.reshape(npool, PAGE * HEADS, HEAD_W),
                       cache_v.reshape(npool, PAGE * HEADS, HEAD_W))
    y_s = _ffn(mix(o_h, gate, o_d, xs, Bs), p_sample.reshape(Bs, PLE_DIM), ffn_w, Bs)

    hd = (HEADS, HEAD_W)
    return (y_p.reshape(Bp, Lp, D), y_s.reshape(Bs, Ls, D),
            k_p.reshape((1, Bp, Lp) + hd), v_p.reshape((1, Bp, Lp) + hd),
            jnp.swapaxes(st_p, -1, -2)[None],
            k_s.reshape((1, Bs, Ls) + hd), v_s.reshape((1, Bs, Ls) + hd), st_s[None])
```

```python
import functools
import math

import numpy as np
import jax
import jax.numpy as jnp
from jax import lax
from jax.experimental import pallas as pl
from jax.experimental.pallas import tpu as pltpu

F32 = jnp.float32
BF16 = jnp.bfloat16
I32 = jnp.int32

D_MODEL = 1024
HEADS = 4
HEAD_W = 128
GROUP_W = HEADS * HEAD_W
DA_DH = 64
ROPE_THETA = 10000.0
HG_CHUNK = 64
PEER_HEADS = 8
PEER_NKEYS = 128
PEER_TOPK = 16
PEER_SEL = PEER_HEADS * PEER_TOPK
PLE_DIM = 256
EPS = 1e-6
PAGE = 128
PAGES_PER_STEP = 16
ROW_CHUNKS = D_MODEL // HEAD_W
WORD_ROWS = ROW_CHUNKS // 2
VMEM_LIMIT = 56 * 1024 * 1024

NT = (((1,), (1,)), ((), ()))
TN = (((0,), (0,)), ((), ()))


def _cparams(sem):
    return pltpu.CompilerParams(dimension_semantics=sem, vmem_limit_bytes=VMEM_LIMIT)


def _tree(op, xs):
    xs = list(xs)
    while len(xs) > 1:
        nxt = [op(xs[i], xs[i + 1]) for i in range(0, len(xs) - 1, 2)]
        if len(xs) % 2:
            nxt.append(xs[-1])
        xs = nxt
    return xs[0]


def _row_reduce(op, lane_reduce, x):
    blocks = [x[:, i * HEAD_W:(i + 1) * HEAD_W] for i in range(x.shape[1] // HEAD_W)]
    return lane_reduce(_tree(op, blocks), axis=-1, keepdims=True)


def _rms(x, g):
    return x * lax.rsqrt(jnp.mean(x * x, axis=-1, keepdims=True) + EPS) * g


def _store_head_rows(ref, x):
    n = x.shape[0]
    for h in range(HEADS):
        ref[pl.ds(h, n, stride=HEADS), :] = x[:, h * HEAD_W:(h + 1) * HEAD_W]


def _proj_kernel(x_ref, g_ref, w_ref, lb_ref, cos_ref, sin_ref,
                 hq_ref, hk_ref, lf_ref, hi_ref, gate_ref, k_ref, v_ref,
                 qb_ref, kb_ref, vb_ref, *, v_transposed):
    xb = _rms(x_ref[...], g_ref[...]).astype(BF16)

    def col(i):
        return jnp.dot(xb, w_ref[:, i * GROUP_W:(i + 1) * GROUP_W], preferred_element_type=F32)

    hq_ref[...] = col(0)
    lb = lb_ref[...]
    f = lb + (1.0 - lb) * jax.nn.sigmoid(col(1))
    lf_ref[...] = jnp.log(f)
    hk_ref[...] = 1.0 - f
    hi_ref[...] = col(2)
    gate_ref[...] = jax.nn.silu(col(3))

    cosf = cos_ref[...]
    sins = sin_ref[...]
    lane = lax.broadcasted_iota(I32, cosf.shape, 1)
    first_half = (lane % DA_DH) < (DA_DH // 2)

    def rope(x):
        swapped = jnp.where(first_half,
                            pltpu.roll(x, GROUP_W - DA_DH // 2, 1),
                            pltpu.roll(x, DA_DH // 2, 1))
        return x * cosf + swapped * sins

    q = rope(col(4))
    qb_ref[...] = (q * (DA_DH ** -0.5)).astype(BF16)
    k = rope(col(5))
    _store_head_rows(k_ref, k)
    kb_ref[...] = k.astype(BF16)
    v = col(6)
    _store_head_rows(v_ref, v)
    vb_ref[...] = (v.T if v_transposed else v).astype(BF16)


def _proj(x, g, w_bf, lb, cosf, sins, tm, v_transposed):
    T = x.shape[0]
    nl = cosf.shape[0] // tm
    row = lambda i: (i, 0)
    fixed = lambda i: (0, 0)
    tab = lambda i: (i % nl, 0)
    f32o = jax.ShapeDtypeStruct((T, GROUP_W), F32)
    bfo = jax.ShapeDtypeStruct((T, GROUP_W), BF16)
    ospec = pl.BlockSpec((tm, GROUP_W), row)
    vspec, vshape = ospec, bfo
    if v_transposed:
        vspec = pl.BlockSpec((GROUP_W, tm), lambda i: (0, i))
        vshape = jax.ShapeDtypeStruct((GROUP_W, T), BF16)
    return pl.pallas_call(
        functools.partial(_proj_kernel, v_transposed=v_transposed),
        grid=(T // tm,),
        in_specs=[pl.BlockSpec((tm, D_MODEL), row),
                  pl.BlockSpec((1, D_MODEL), fixed),
                  pl.BlockSpec(w_bf.shape, fixed),
                  pl.BlockSpec((1, GROUP_W), fixed),
                  pl.BlockSpec((tm, GROUP_W), tab),
                  pl.BlockSpec((tm, GROUP_W), tab)],
        out_specs=[ospec] * 5 + [pl.BlockSpec((tm * HEADS, HEAD_W), row)] * 2 + [ospec] * 2 + [vspec],
        out_shape=[f32o] * 5 + [jax.ShapeDtypeStruct((T * HEADS, HEAD_W), F32)] * 2 + [bfo] * 2 + [vshape],
        compiler_params=_cparams(("parallel",)),
        name="proj",
    )(x, g, w_bf, lb, cosf, sins)


def _hgrn_consts():
    C = HG_CHUNK
    t = np.arange(C)[:, None]
    u = np.arange(C)[None, :]
    mats = [(u <= t)]
    lows, pms = [], []
    h = C // 2
    while h >= 1:
        base = (t // (2 * h)) * (2 * h)
        r = base + h - 1
        lower = t >= base + h
        m = np.where(lower, (u > r) & (u <= t), (u > t) & (u <= r))
        mats.append(m)
        lows.append(np.broadcast_to(lower, (C, HEAD_W)))
        s = np.arange(C)[None, :]
        pms.append((t // (2 * h)) == (s // (2 * h)))
        h //= 2
    mats.append(u > t)
    wall = np.concatenate(mats, axis=0).astype(np.float32)
    return (jnp.asarray(wall, BF16), jnp.asarray(np.stack(lows), F32), jnp.asarray(np.stack(pms), F32))


def _split3(x):
    a = x.astype(BF16)
    r = x - a.astype(F32)
    b = r.astype(BF16)
    c = (r - b.astype(F32)).astype(BF16)
    return a, b, c


HGRN_HEADS_PER_STEP = 2


def _hgrn_kernel(q_ref, k_ref, v_ref, lf_ref, wall_ref, low_ref, pm_ref, o_ref, st_ref, st_scr, *, nlev):
    i = pl.program_id(2)
    C = HG_CHUNK
    nh = st_scr.shape[0]

    @pl.when(i == 0)
    def _():
        st_scr[...] = jnp.zeros_like(st_scr)

    wall = wall_ref[...]
    eye = (lax.broadcasted_iota(I32, (C, C), 0) == lax.broadcasted_iota(I32, (C, C), 1)).astype(F32)

    def chunk(q, k, v, lf, st):
        lcat = jnp.concatenate(_split3(lf), axis=-1)
        d3 = jnp.dot(wall, lcat, preferred_element_type=F32)
        e_all = jnp.exp(d3[:, :HEAD_W] + d3[:, HEAD_W:2 * HEAD_W] + d3[:, 2 * HEAD_W:])
        e_g = e_all[0:C]
        e_last = e_all[C - 1:C]
        e_k = e_all[(nlev + 1) * C:(nlev + 2) * C]
        vb = v.astype(BF16)
        o = lax.dot_general((q * e_g).astype(BF16), st.astype(BF16), NT, preferred_element_type=F32)
        a = eye * jnp.sum(q * k, axis=-1, keepdims=True)
        for l in range(nlev):
            e_l = e_all[(l + 1) * C:(l + 2) * C]
            low = low_ref[l]
            ql = (q * e_l * low).astype(BF16)
            kl = (k * e_l * (1.0 - low)).astype(BF16)
            a = a + lax.dot_general(ql, kl, NT, preferred_element_type=F32) * pm_ref[l]
        o = o + jnp.dot(a.astype(BF16), vb, preferred_element_type=F32)
        kd = (k * e_k).astype(BF16)
        return o, e_last * st + lax.dot_general(vb, kd, TN, preferred_element_type=F32)

    sts = [st_scr[hh] for hh in range(nh)]
    for c in range(q_ref.shape[0] // C):
        sl = slice(c * C, (c + 1) * C)
        for hh in range(nh):
            cols = slice(hh * HEAD_W, (hh + 1) * HEAD_W)
            o, sts[hh] = chunk(q_ref[sl, cols], k_ref[sl, cols], v_ref[sl, cols], lf_ref[sl, cols], sts[hh])
            o_ref[sl, cols] = o
    for hh in range(nh):
        st_scr[hh] = sts[hh]

    @pl.when(i == pl.num_programs(2) - 1)
    def _():
        for hh in range(nh):
            st_ref[0, hh] = sts[hh]


def _hgrn_prompt(hq, hk, hi, lf, B, L, lb_rows):
    wall, low, pm = _hgrn_consts()
    nlev = low.shape[0]
    nblk = L // lb_rows
    nh = HGRN_HEADS_PER_STEP
    blk = pl.BlockSpec((lb_rows, nh * HEAD_W), lambda b, h, i: (b * nblk + i, h))
    cst = lambda a: pl.BlockSpec(a.shape, lambda b, h, i: (0,) * a.ndim)
    return pl.pallas_call(
        functools.partial(_hgrn_kernel, nlev=nlev),
        grid=(B, HEADS // nh, nblk),
        in_specs=[blk, blk, blk, blk, cst(wall), cst(low), cst(pm)],
        out_specs=[blk, pl.BlockSpec((1, nh, HEAD_W, HEAD_W), lambda b, h, i: (b, h, 0, 0))],
        out_shape=[jax.ShapeDtypeStruct(hq.shape, F32),
                   jax.ShapeDtypeStruct((B, HEADS, HEAD_W, HEAD_W), F32)],
        scratch_shapes=[pltpu.VMEM((nh, HEAD_W, HEAD_W), F32)],
        compiler_params=_cparams(("parallel", "parallel", "arbitrary")),
        name="hgrn_prompt",
    )(hq, hk, hi, lf, wall, low, pm)


def _hgrn_step_kernel(qc_ref, kc_ref, lfc_ref, v_ref, s0_ref, o_ref, s_ref):
    qc, kc = qc_ref[0, 0], kc_ref[0, 0]
    dec = jnp.exp(lfc_ref[0, 0])
    v = v_ref[0, 0]
    s0 = s0_ref[0, 0]
    s_ref[0, 0] = dec * s0 + kc * v
    o_ref[0, 0] = (jnp.sum((qc * dec) * s0, axis=0, keepdims=True)
                   + jnp.sum(qc * kc, axis=0, keepdims=True) * v)


def _hgrn_step(hq, hk, hi, lf, s0):
    B = hq.shape[0]
    colv = lambda a: a.reshape(B, HEADS, HEAD_W, 1)
    cspec = pl.BlockSpec((1, 1, HEAD_W, 1), lambda b, h: (b, h, 0, 0))
    rspec = pl.BlockSpec((1, 1, 1, HEAD_W), lambda b, h: (b, h, 0, 0))
    sspec = pl.BlockSpec((1, 1, HEAD_W, HEAD_W), lambda b, h: (b, h, 0, 0))
    o, s = pl.pallas_call(
        _hgrn_step_kernel,
        grid=(B, HEADS),
        in_specs=[cspec, cspec, cspec, rspec, sspec],
        out_specs=[rspec, sspec],
        out_shape=[jax.ShapeDtypeStruct((B, HEADS, 1, HEAD_W), F32),
                   jax.ShapeDtypeStruct((B, HEADS, HEAD_W, HEAD_W), F32)],
        compiler_params=_cparams(("parallel", "parallel")),
        name="hgrn_step",
    )(colv(hq), colv(hk), colv(lf), hi.reshape(B, HEADS, 1, HEAD_W), s0)
    return o.reshape(B, GROUP_W), s


ATTN_QW = 128


def _attn_kernel(lam_ref, q_ref, k_ref, vt_ref, o_ref, qs_scr, m_scr, l_scr, acc_scr):
    qi = pl.program_id(1)
    ki = pl.program_id(2)
    bq = q_ref.shape[0]
    bk = k_ref.shape[0]
    nsub = 2 * bq // ATTN_QW

    @pl.when(ki == 0)
    def _():
        lane = lax.broadcasted_iota(I32, (bq, HEAD_W), 1)
        for h in range(HEADS):
            q = q_ref[:, h * HEAD_W:(h + 1) * HEAD_W]
            zero = jnp.zeros_like(q)
            qs_scr[h] = jnp.concatenate([jnp.where(lane < DA_DH, q, zero),
                                         jnp.where(lane >= DA_DH, q, zero)], axis=0)
        m_scr[...] = jnp.full(m_scr.shape, -jnp.inf, F32)
        l_scr[...] = jnp.zeros_like(l_scr)
        acc_scr[...] = jnp.zeros_like(acc_scr)

    @pl.when(ki <= qi)
    def _():
        kpos = lax.broadcasted_iota(I32, (bk, ATTN_QW), 0)
        qpos = lax.broadcasted_iota(I32, (bk, ATTN_QW), 1)
        off_diag = ki < qi
        for h in range(HEADS):
            kh = k_ref[:, h * HEAD_W:(h + 1) * HEAD_W]
            vth = vt_ref[h * HEAD_W:(h + 1) * HEAD_W, :]
            for sb in range(nsub):
                cols = slice(sb * ATTN_QW, (sb + 1) * ATTN_QW)
                st = lax.dot_general(kh, qs_scr[h, cols, :], NT, preferred_element_type=F32)
                keep = (kpos <= qpos + (sb * ATTN_QW) % bq) | off_diag
                st = jnp.where(keep, st, -jnp.inf)
                m_old = m_scr[h, :, cols]
                smax = _tree(jnp.maximum, [st[r * 8:(r + 1) * 8] for r in range(bk // 8)])
                m_new = jnp.maximum(m_old, jnp.max(smax, axis=0, keepdims=True))
                alpha = jnp.exp(m_old - m_new)
                pt = jnp.exp(st - m_new)
                psum = _tree(jnp.add, [pt[r * 8:(r + 1) * 8] for r in range(bk // 8)])
                l_scr[h, :, cols] = alpha * l_scr[h, :, cols] + jnp.sum(psum, axis=0, keepdims=True)
                acc_scr[h, :, cols] = (alpha * acc_scr[h, :, cols]
                                       + jnp.dot(vth, pt.astype(BF16), preferred_element_type=F32))
                m_scr[h, :, cols] = m_new

    @pl.when(ki == qi)
    def _():
        lam = lam_ref[0]
        for h in range(HEADS):
            r = acc_scr[h] / l_scr[h]
            o_ref[:, h * HEAD_W:(h + 1) * HEAD_W] = (r[:, :bq] - lam * r[:, bq:]).T


def _attn_prompt(lam, qb, kb, vbt, B, L, bq):
    nq = L // bq
    qspec = pl.BlockSpec((bq, GROUP_W), lambda b, qi, ki: (b * nq + qi, 0))
    kspec = pl.BlockSpec((bq, GROUP_W), lambda b, qi, ki: (b * nq + jnp.minimum(ki, qi), 0))
    vspec = pl.BlockSpec((GROUP_W, bq), lambda b, qi, ki: (0, b * nq + jnp.minimum(ki, qi)))
    return pl.pallas_call(
        _attn_kernel,
        grid=(B, nq, nq),
        in_specs=[pl.BlockSpec(memory_space=pltpu.SMEM), qspec, kspec, vspec],
        out_specs=qspec,
        out_shape=jax.ShapeDtypeStruct(qb.shape, F32),
        scratch_shapes=[pltpu.VMEM((HEADS, 2 * bq, HEAD_W), BF16),
                        pltpu.VMEM((HEADS, 1, 2 * bq), F32),
                        pltpu.VMEM((HEADS, 1, 2 * bq), F32),
                        pltpu.VMEM((HEADS, HEAD_W, 2 * bq), F32)],
        compiler_params=_cparams(("parallel", "parallel", "arbitrary")),
        name="attn_prompt",
    )(lam, qb, kb, vbt)


def _decode_kernel(pt_ref, lam_ref, q_ref, kn_ref, vn_ref, *rest):
    npg = PAGES_PER_STEP
    k_refs, v_refs = rest[:npg], rest[npg:2 * npg]
    o_ref, kcat, vcat, m_scr, l_scr, acc_scr = rest[2 * npg:]
    i = pl.program_id(1)
    nrow = 2 * HEADS

    rowi = lax.broadcasted_iota(I32, (nrow, GROUP_W), 0)
    lane = lax.broadcasted_iota(I32, (nrow, GROUP_W), 1)
    sel = (lane // HEAD_W == rowi // 2) & ((lane % HEAD_W) // DA_DH == rowi % 2)
    qbd = jnp.where(sel, jnp.broadcast_to(q_ref[0], (nrow, GROUP_W)), 0.0)

    @pl.when(i == 0)
    def _():
        m_scr[...] = jnp.full(m_scr.shape, -jnp.inf, F32)
        l_scr[...] = jnp.zeros_like(l_scr)
        acc_scr[...] = jnp.zeros_like(acc_scr)

    for p in range(npg):
        for h in range(HEADS):
            rows = slice(p * PAGE, (p + 1) * PAGE)
            cols = slice(h * HEAD_W, (h + 1) * HEAD_W)
            kcat[rows, cols] = k_refs[p][0, pl.ds(h, PAGE, stride=HEADS), :].astype(BF16)
            vcat[rows, cols] = v_refs[p][0, pl.ds(h, PAGE, stride=HEADS), :].astype(BF16)
    s = lax.dot_general(qbd.astype(BF16), kcat[...], NT, preferred_element_type=F32)
    m_old = m_scr[...]
    m_new = jnp.maximum(m_old, _row_reduce(jnp.maximum, jnp.max, s))
    alpha = jnp.exp(m_old - m_new)
    p_ = jnp.exp(s - m_new)
    l_scr[...] = alpha * l_scr[...] + _row_reduce(jnp.add, jnp.sum, p_)
    acc_scr[...] = alpha * acc_scr[...] + jnp.dot(p_.astype(BF16), vcat[...], preferred_element_type=F32)
    m_scr[...] = m_new

    @pl.when(i == pl.num_programs(1) - 1)
    def _():
        s_new = jnp.sum(qbd * kn_ref[0], axis=-1, keepdims=True)
        m_o = m_scr[...]
        m_n = jnp.maximum(m_o, s_new)
        al = jnp.exp(m_o - m_n)
        pn = jnp.exp(s_new - m_n)
        l_f = al * l_scr[...] + pn
        r = (al * acc_scr[...] + pn * vn_ref[0]) / l_f
        lam = lam_ref[0]
        outs = []
        for h in range(HEADS):
            blk = slice(h * HEAD_W, (h + 1) * HEAD_W)
            outs.append(r[2 * h:2 * h + 1, blk] - lam * r[2 * h + 1:2 * h + 2, blk])
        o_ref[0] = jnp.concatenate(outs, axis=-1)


def _attn_decode(page_table, lam, q, kn, vn, ck, cv):
    B = q.shape[0]
    nsteps = page_table.shape[1] // PAGES_PER_STEP
    tok = pl.BlockSpec((1, 1, GROUP_W), lambda b, i, pt: (b, 0, 0))

    def page(p):
        return pl.BlockSpec((1, PAGE * HEADS, HEAD_W),
                            lambda b, i, pt: (pt[b, i * PAGES_PER_STEP + p], 0, 0))

    pages = [page(p) for p in range(PAGES_PER_STEP)]
    grid_spec = pltpu.PrefetchScalarGridSpec(
        num_scalar_prefetch=1,
        grid=(B, nsteps),
        in_specs=[pl.BlockSpec(memory_space=pltpu.SMEM), tok, tok, tok] + pages + pages,
        out_specs=tok,
        scratch_shapes=[pltpu.VMEM((PAGES_PER_STEP * PAGE, GROUP_W), BF16),
                        pltpu.VMEM((PAGES_PER_STEP * PAGE, GROUP_W), BF16),
                        pltpu.VMEM((2 * HEADS, 1), F32),
                        pltpu.VMEM((2 * HEADS, 1), F32),
                        pltpu.VMEM((2 * HEADS, GROUP_W), F32)])
    r3 = lambda a: a.reshape(B, 1, GROUP_W)
    out = pl.pallas_call(
        _decode_kernel,
        grid_spec=grid_spec,
        out_shape=jax.ShapeDtypeStruct((B, 1, GROUP_W), F32),
        compiler_params=_cparams(("parallel", "arbitrary")),
        name="attn_decode",
    )(page_table, lam, r3(q), r3(kn), r3(vn), *([ck] * PAGES_PER_STEP), *([cv] * PAGES_PER_STEP))
    return out.reshape(B, GROUP_W)


def _mix_kernel(oh_ref, gate_ref, od_ref, x_ref, gh_ref, gd_ref, wo_ref, gf_ref, wq_ref,
                hp_ref, hn_ref, qh_ref, *, dscale):
    parts = []
    for h in range(HEADS):
        blk = slice(h * HEAD_W, (h + 1) * HEAD_W)
        parts.append(_rms(oh_ref[:, blk], gh_ref[...]) * gate_ref[:, blk])
    for h in range(HEADS):
        blk = slice(h * HEAD_W, (h + 1) * HEAD_W)
        parts.append(_rms(od_ref[:, blk], gd_ref[...]) * dscale)
    y = jnp.concatenate(parts, axis=-1).astype(BF16)
    hp = x_ref[...] + jnp.dot(y, wo_ref[...], preferred_element_type=F32)
    hp_ref[...] = hp
    hn = _rms(hp, gf_ref[...])
    hn_ref[...] = hn
    qh_ref[...] = jnp.dot(hn.astype(BF16), wq_ref[...], preferred_element_type=F32).astype(BF16)


def _mix(oh, gate, od, x, gh, gd, wo_bf, gf, wq_bf, dscale, tm):
    T = x.shape[0]
    row = lambda i: (i, 0)
    fixed = lambda i: (0, 0)
    half = pl.BlockSpec((tm, GROUP_W), row)
    full = pl.BlockSpec((tm, D_MODEL), row)
    cst = lambda a: pl.BlockSpec(a.shape, fixed)
    nq = wq_bf.shape[1]
    return pl.pallas_call(
        functools.partial(_mix_kernel, dscale=dscale),
        grid=(T // tm,),
        in_specs=[half, half, half, full, cst(gh), cst(gd), cst(wo_bf), cst(gf), cst(wq_bf)],
        out_specs=[full, full, pl.BlockSpec((tm, nq), row)],
        out_shape=[jax.ShapeDtypeStruct((T, D_MODEL), F32), jax.ShapeDtypeStruct((T, D_MODEL), F32),
                   jax.ShapeDtypeStruct((T, nq), BF16)],
        compiler_params=_cparams(("parallel",)),
        name="mix",
    )(oh, gate, od, x, gh, gd, wo_bf, gf, wq_bf)


def _staircase():
    K = PEER_TOPK
    return [(a, b) for a in range(K) for b in range(K) if (a + 1) * (b + 1) <= K]


def _topk_kernel(qh_ref, sk_ref, e_ref, g_ref, s_scr, val_scr, idx_scr, cand_scr, cidx_scr, sc_scr, e_scr):
    K = PEER_TOPK
    NK = PEER_NKEYS
    tb = qh_ref.shape[0]
    half_w = PEER_HEADS * NK
    neg = -jnp.inf

    for c in range(2):
        qc = qh_ref[:, c * half_w:(c + 1) * half_w]
        s = lax.dot_general(sk_ref[c], qc, NT, preferred_element_type=F32)
        s_scr[...] = s.reshape(NK, PEER_HEADS, tb)

        def body(a, carry):
            sv = [s_scr[n] for n in range(NK)]
            m = _tree(jnp.maximum, sv)
            idx = _tree(jnp.minimum, [jnp.where(sv[n] == m, float(n), float(NK)) for n in range(NK)])
            for n in range(NK):
                s_scr[n] = jnp.where(idx == float(n), neg, sv[n])
            val_scr[c, a] = m
            idx_scr[c, a] = idx.astype(I32)
            return carry

        lax.fori_loop(0, K, body, 0)

    pairs = _staircase()
    for i, (a, b) in enumerate(pairs):
        cand_scr[i] = val_scr[0, a] + val_scr[1, b]
        cidx_scr[i] = (idx_scr[0, a] * NK + idx_scr[1, b]) * WORD_ROWS
    flats = [float(a * K + b) for a, b in pairs]
    big = float(K * K)

    def body2(r, carry):
        cs = [cand_scr[i] for i in range(len(pairs))]
        m = _tree(jnp.maximum, cs)
        pos = _tree(jnp.minimum, [jnp.where(cv == m, fl, big) for cv, fl in zip(cs, flats)])
        picks = []
        for i, (cv, fl) in enumerate(zip(cs, flats)):
            hit = pos == fl
            cand_scr[i] = jnp.where(hit, neg, cv)
            picks.append(jnp.where(hit, cidx_scr[i], 0))
        sc_scr[r] = m
        e_scr[r] = _tree(jnp.maximum, picks)
        return carry

    lax.fori_loop(0, K, body2, 0)
    sc = sc_scr[...]
    ex = jnp.exp(sc - sc[0:1])
    g = ex / jnp.sum(ex, axis=0, keepdims=True)
    g_ref[...] = g.reshape(PEER_SEL, tb).T
    e_ref[...] = e_scr[...].reshape(PEER_SEL, tb).T


def _topk(qh, sk_big, tb):
    T = qh.shape[0]
    ncand = len(_staircase())
    hw = (PEER_HEADS, tb)
    return pl.pallas_call(
        _topk_kernel,
        grid=(T // tb,),
        in_specs=[pl.BlockSpec((tb, qh.shape[1]), lambda i: (i, 0)),
                  pl.BlockSpec(sk_big.shape, lambda i: (0, 0, 0))],
        out_specs=[pl.BlockSpec((tb, PEER_SEL), lambda i: (i, 0))] * 2,
        out_shape=[jax.ShapeDtypeStruct((T, PEER_SEL), I32),
                   jax.ShapeDtypeStruct((T, PEER_SEL), F32)],
        scratch_shapes=[pltpu.VMEM((PEER_NKEYS,) + hw, F32),
                        pltpu.VMEM((2, PEER_TOPK) + hw, F32),
                        pltpu.VMEM((2, PEER_TOPK) + hw, I32),
                        pltpu.VMEM((ncand,) + hw, F32),
                        pltpu.VMEM((ncand,) + hw, I32),
                        pltpu.VMEM((PEER_TOPK,) + hw, F32),
                        pltpu.VMEM((PEER_TOPK,) + hw, I32)],
        compiler_params=_cparams(("parallel",)),
        name="peer_topk",
    )(qh, sk_big)


def _pack_kernel(t_ref, o_ref):
    n = t_ref.shape[0]

    def bits(x):
        return pltpu.bitcast(x.astype(BF16).astype(F32), jnp.uint32) >> 16

    for s in range(WORD_ROWS):
        lo = bits(t_ref[:, (2 * s) * HEAD_W:(2 * s + 1) * HEAD_W])
        hi = bits(t_ref[:, (2 * s + 1) * HEAD_W:(2 * s + 2) * HEAD_W])
        o_ref[pl.ds(s, n, stride=WORD_ROWS), :] = pltpu.bitcast((hi << 16) | lo, I32)


def _pack_table(tab, rows=256):
    n = tab.shape[0]
    return pl.pallas_call(
        _pack_kernel,
        grid=(n // rows,),
        in_specs=[pl.BlockSpec((rows, D_MODEL), lambda i: (i, 0))],
        out_specs=pl.BlockSpec((rows * WORD_ROWS, HEAD_W), lambda i: (i, 0)),
        out_shape=jax.ShapeDtypeStruct((n * WORD_ROWS, HEAD_W), I32),
        compiler_params=_cparams(("parallel",)),
        name="pack_table",
    )(tab)


IDX_GROUP = 32
IDX_SLOTS = 2
STG_BUFS = 2


def _stream_index_groups(idx_hbm, idx_smem, sem, tb, n_groups, group_fn):
    step = pl.program_id(0)
    grp = idx_smem.shape[1]
    gps = tb // grp
    slots = min(IDX_SLOTS, gps)
    assert gps % slots == 0

    def copy(group, slot):
        return pltpu.make_async_copy(idx_hbm.at[pl.ds(group * grp, grp)],
                                     idx_smem.at[slot], sem.at[slot])

    @pl.when(step == 0)
    def _():
        for s in range(slots):
            copy(s, s).start()

    def body(it, carry):
        for s in range(slots):
            gl = it * slots + s
            g = step * gps + gl
            copy(g, s).wait()
            group_fn(gl, idx_smem.at[s])

            @pl.when(g + slots < n_groups)
            def _():
                copy(g + slots, s).start()
        return carry

    lax.fori_loop(0, gps // slots, body, 0)


def _gather_rows(idx8, i, tab_ref, stg):
    for j in range(PEER_SEL):
        r = pl.multiple_of(idx8[i, j], WORD_ROWS)
        stg[j * WORD_ROWS:(j + 1) * WORD_ROWS, :] = tab_ref[pl.ds(r, WORD_ROWS), :]


def _peer_u_kernel(idx_hbm, x_ref, g_ref, tab_ref, dmask_ref, gsum_ref, expand_ref, w_ref,
                   stg, c_scr, idx_smem, sem, *, n_groups):
    tb = x_ref.shape[0]
    grp = idx_smem.shape[1]
    dmask = dmask_ref[...]

    def group(gl, idx8):
        for sub in range(grp // 8):
            rows = []
            for i in range(sub * 8, sub * 8 + 8):
                buf = stg.at[i % STG_BUFS]
                _gather_rows(idx8, i, tab_ref, buf)
                ub = pltpu.bitcast(buf[...], BF16)
                xt = x_ref[gl * grp +i].astype(BF16)
                r = lax.dot_general(xt, ub, NT, preferred_element_type=F32)
                rows.append(jnp.sum(r * dmask, axis=0, keepdims=True))
            row0 = pl.multiple_of(gl * grp +sub * 8, 8)
            c_scr[pl.ds(row0, 8), :] = jnp.concatenate(rows, axis=0)

    _stream_index_groups(idx_hbm, idx_smem, sem, tb, n_groups, group)
    c = c_scr[...]
    c_hi = c.astype(BF16)
    c_lo = (c - c_hi.astype(F32)).astype(BF16)
    a = (jnp.dot(c_hi, gsum_ref[...], preferred_element_type=F32)
         + jnp.dot(c_lo, gsum_ref[...], preferred_element_type=F32))
    w = g_ref[...] * jax.nn.gelu(a)
    w_ref[...] = jnp.dot(w.astype(BF16), expand_ref[...], preferred_element_type=F32)


def _peer_v_kernel(idx_hbm, w_ref, tab_ref, dmask_ref, o_ref, stg, idx_smem, sem, *, n_groups):
    tb = w_ref.shape[0]
    grp = idx_smem.shape[1]
    dmask = dmask_ref[...]

    def group(gl, idx8):
        for sub in range(grp // 8):
            w8 = w_ref[pl.ds(pl.multiple_of(gl * grp +sub * 8, 8), 8), :]
            for k in range(8):
                i = sub * 8 + k
                buf = stg.at[i % STG_BUFS]
                _gather_rows(idx8, i, tab_ref, buf)
                vb = pltpu.bitcast(buf[...], BF16)
                lhs = (jnp.broadcast_to(w8[k:k + 1, :], dmask.shape) * dmask).astype(BF16)
                o_ref[gl * grp +i] = jnp.dot(lhs, vb, preferred_element_type=F32)

    _stream_index_groups(idx_hbm, idx_smem, sem, tb, n_groups, group)


def _peer_consts():
    sel_w = PEER_SEL * ROW_CHUNKS
    lane = np.arange(sel_w)
    dmask = (lane[None, :] % ROW_CHUNKS == np.arange(ROW_CHUNKS)[:, None]).astype(np.float32)
    gsum = (lane[:, None] // ROW_CHUNKS == np.arange(PEER_SEL)[None, :]).astype(np.float32)
    return jnp.asarray(dmask, F32), jnp.asarray(gsum, BF16), jnp.asarray(gsum.T, BF16)


def _peer_u(idx, hn, g, tab_u, tb):
    T = hn.shape[0]
    dmask, gsum, expand = _peer_consts()
    sel_w = PEER_SEL * ROW_CHUNKS
    cst = lambda a: pl.BlockSpec(a.shape, lambda i: (0, 0))
    return pl.pallas_call(
        functools.partial(_peer_u_kernel, n_groups=T // min(IDX_GROUP, tb)),
        grid=(T // tb,),
        in_specs=[pl.BlockSpec(memory_space=pl.ANY),
                  pl.BlockSpec((tb, ROW_CHUNKS, HEAD_W), lambda i: (i, 0, 0)),
                  pl.BlockSpec((tb, PEER_SEL), lambda i: (i, 0)),
                  pl.BlockSpec(memory_space=pltpu.VMEM),
                  cst(dmask), cst(gsum), cst(expand)],
        out_specs=pl.BlockSpec((tb, sel_w), lambda i: (i, 0)),
        out_shape=jax.ShapeDtypeStruct((T, sel_w), F32),
        scratch_shapes=[pltpu.VMEM((STG_BUFS, PEER_SEL * WORD_ROWS, HEAD_W), I32),
                        pltpu.VMEM((tb, sel_w), F32),
                        pltpu.SMEM((IDX_SLOTS, min(IDX_GROUP, tb), PEER_SEL), I32),
                        pltpu.SemaphoreType.DMA((IDX_SLOTS,))],
        compiler_params=_cparams(("arbitrary",)),
        name="peer_u",
    )(idx, hn.reshape(T, ROW_CHUNKS, HEAD_W), g, tab_u, dmask, gsum, expand)


def _peer_v(idx, wexp, tab_v, tb):
    T = wexp.shape[0]
    dmask, _, _ = _peer_consts()
    out = pl.pallas_call(
        functools.partial(_peer_v_kernel, n_groups=T // min(IDX_GROUP, tb)),
        grid=(T // tb,),
        in_specs=[pl.BlockSpec(memory_space=pl.ANY),
                  pl.BlockSpec((tb, wexp.shape[1]), lambda i: (i, 0)),
                  pl.BlockSpec(memory_space=pltpu.VMEM),
                  pl.BlockSpec(dmask.shape, lambda i: (0, 0))],
        out_specs=pl.BlockSpec((tb, ROW_CHUNKS, HEAD_W), lambda i: (i, 0, 0)),
        out_shape=jax.ShapeDtypeStruct((T, ROW_CHUNKS, HEAD_W), F32),
        scratch_shapes=[pltpu.VMEM((STG_BUFS, PEER_SEL * WORD_ROWS, HEAD_W), I32),
                        pltpu.SMEM((IDX_SLOTS, min(IDX_GROUP, tb), PEER_SEL), I32),
                        pltpu.SemaphoreType.DMA((IDX_SLOTS,))],
        compiler_params=_cparams(("arbitrary",)),
        name="peer_v",
    )(idx, wexp, tab_v, dmask)
    return out.reshape(T, D_MODEL)


def _ple_kernel(hp_ref, pe_ref, p_ref, wg_ref, bg_ref, wp_ref, gf_ref, y_ref):
    h = hp_ref[...] + pe_ref[...]
    gate = jax.nn.sigmoid(jnp.dot(h.astype(BF16), wg_ref[...], preferred_element_type=F32) + bg_ref[...])
    h = h + gate * jnp.dot(p_ref[...].astype(BF16), wp_ref[...], preferred_element_type=F32)
    y_ref[...] = _rms(h, gf_ref[...])


def _ple(hp, pe, p, wg_bf, bg, wp_bf, gf, tm):
    T = hp.shape[0]
    row = lambda i: (i, 0)
    full = pl.BlockSpec((tm, D_MODEL), row)
    cst = lambda a: pl.BlockSpec(a.shape, lambda i: (0, 0))
    return pl.pallas_call(
        _ple_kernel,
        grid=(T // tm,),
        in_specs=[full, full, pl.BlockSpec((tm, PLE_DIM), row), cst(wg_bf), cst(bg), cst(wp_bf), cst(gf)],
        out_specs=full,
        out_shape=jax.ShapeDtypeStruct((T, D_MODEL), F32),
        compiler_params=_cparams(("parallel",)),
        name="ple",
    )(hp, pe, p, wg_bf, bg, wp_bf, gf)


def _rope_tables(pos):
    half = DA_DH // 2
    freqs = ROPE_THETA ** (-jnp.arange(half, dtype=F32) / half)
    ang = pos.astype(F32)[:, None] * freqs[None, :]
    cos = jnp.tile(jnp.cos(ang), (1, GROUP_W // half))
    sign = jnp.where((jnp.arange(GROUP_W) % DA_DH) < half, -1.0, 1.0).astype(F32)
    sin = jnp.tile(jnp.sin(ang), (1, GROUP_W // half)) * sign[None, :]
    return cos, sin


def _ffn(hp_parts, p, weights, tb_peer):
    hp, hn, qh = hp_parts
    (sk_big, tab_u, tab_v, wg_bf, bg, wp_bf, gfinal) = weights
    T = hp.shape[0]
    tk = 256
    tpad = -(-T // tk) * tk
    qh_p = jnp.pad(qh, ((0, tpad - T), (0, 0))) if tpad != T else qh
    idx, gw = _topk(qh_p, sk_big, tk)
    idx, gw = idx[:T], gw[:T]
    wexp = _peer_u(idx, hn, gw, tab_u, tb_peer)
    pe = _peer_v(idx, wexp, tab_v, tb_peer)
    tm = min(256, T)
    return _ple(hp, pe, p, wg_bf, bg, wp_bf, gfinal, tm)


def kernel(x_prompt, x_sample, p_prompt, p_sample, cache_k, cache_v, state_hgrn, page_table, g_attn, w_in, hgrn_gamma, g_hgrn_norm, lambda_q1, lambda_k1, lambda_q2, lambda_k2, g_diff_norm, w_out, g_ffn, peer_w_query, peer_sub_keys, peer_u, peer_v, ple_w_gate, ple_b_gate, ple_w_proj, g_final):
    Bp, Lp, D = x_prompt.shape
    Bs, Ls, _ = x_sample.shape
    assert D == D_MODEL and Ls == 1 and w_in.shape[0] == 1
    l = 0
    past_len = page_table.shape[1] * cache_k.shape[2]
    lam_init = 0.8 - 0.6 * math.exp(-0.3 * l)
    lam = (jnp.exp(jnp.sum(lambda_q1[l] * lambda_k1[l])) - jnp.exp(jnp.sum(lambda_q2[l] * lambda_k2[l]))
           + lam_init).reshape(1).astype(F32)
    lb = jnp.cumsum(jax.nn.softmax(hgrn_gamma.astype(F32), axis=0), axis=0)[l].reshape(1, GROUP_W)

    w_in_bf = w_in[l].astype(BF16)
    w_out_bf = w_out[l].astype(BF16)
    nqc = PEER_HEADS * 2 * PEER_NKEYS
    wq_bf = (peer_w_query[l].reshape(D, PEER_HEADS, 2, PEER_NKEYS).transpose(0, 2, 1, 3)
             .reshape(D, nqc).astype(BF16))
    sk_big = jnp.einsum('hcnk,hg->cnhgk', peer_sub_keys[l], jnp.eye(PEER_HEADS, dtype=F32)).reshape(
        2, PEER_NKEYS * PEER_HEADS, PEER_HEADS * PEER_NKEYS).astype(BF16)
    tab_u = _pack_table(peer_u[l])
    tab_v = _pack_table(peer_v[l])
    wg_bf = ple_w_gate[l].astype(BF16)
    wp_bf = ple_w_proj[l].astype(BF16)
    row = lambda a: a.reshape(1, -1).astype(F32)
    ffn_w = (sk_big, tab_u, tab_v, wg_bf, row(ple_b_gate[l]), wp_bf, row(g_final))

    def group(x, pos_tab, tm, v_transposed):
        cosf, sins = pos_tab
        return _proj(x, row(g_attn[l]), w_in_bf, lb, cosf, sins, tm, v_transposed)

    def mix(oh, gate, od, x, tm):
        return _mix(oh, gate, od, x, row(g_hgrn_norm[l]), row(g_diff_norm[l]), w_out_bf, row(g_ffn[l]),
                    wq_bf, 1.0 - lam_init, tm)

    Tp = Bp * Lp
    xp = x_prompt.reshape(Tp, D)
    tm_p = math.gcd(Lp, 512)
    hq, hk, lf, hi, gate, k_p, v_p, qb, kb, vb = group(xp, _rope_tables(jnp.arange(Lp)), tm_p, True)
    o_h, st_p = _hgrn_prompt(hq, hk, hi, lf, Bp, Lp, math.gcd(Lp, 512))
    o_d = _attn_prompt(lam, qb, kb, vb, Bp, Lp, math.gcd(Lp, 512))
    y_p = _ffn(mix(o_h, gate, o_d, xp, min(256, Tp)), p_prompt.reshape(Tp, PLE_DIM), ffn_w, min(128, Tp))

    xs = x_sample.reshape(Bs, D)
    pos_s = jnp.full((Bs,), past_len, dtype=jnp.int32)
    hq, hk, lf, hi, gate, k_s, v_s, qb, kb, vb = group(xs, _rope_tables(pos_s), Bs, False)
    o_h, st_s = _hgrn_step(hq, hk, hi, lf, state_hgrn.reshape(Bs, HEADS, HEAD_W, HEAD_W))
    npool = cache_k.shape[1]
    o_d = _attn_decode(page_table, lam, qb.astype(F32), k_s.reshape(Bs, GROUP_W), v_s.reshape(Bs, GROUP_W),
                       cache_k.reshape(npool, PAGE * HEADS, HEAD_W),
                       cache_v.reshape(npool, PAGE * HEADS, HEAD_W))
    y_s = _ffn(mix(o_h, gate, o_d, xs, Bs), p_sample.reshape(Bs, PLE_DIM), ffn_w, Bs)

    hd = (HEADS, HEAD_W)
    return (y_p.reshape(Bp, Lp, D), y_s.reshape(Bs, Ls, D),
            k_p.reshape((1, Bp, Lp) + hd), v_p.reshape((1, Bp, Lp) + hd),
            jnp.swapaxes(st_p, -1, -2)[None],
            k_s.reshape((1, Bs, Ls) + hd), v_s.reshape((1, Bs, Ls) + hd), st_s[None])
```

```python
import functools
import math

import numpy as np
import jax
import jax.numpy as jnp
from jax import lax
from jax.experimental import pallas as pl
from jax.experimental.pallas import tpu as pltpu

F32 = jnp.float32
BF16 = jnp.bfloat16
I32 = jnp.int32

D_MODEL = 1024
HEADS = 4
HEAD_W = 128
GROUP_W = HEADS * HEAD_W
DA_DH = 64
ROPE_THETA = 10000.0
HG_CHUNK = 64
PEER_HEADS = 8
PEER_NKEYS = 128
PEER_TOPK = 16
PEER_SEL = PEER_HEADS * PEER_TOPK
PLE_DIM = 256
EPS = 1e-6
PAGE = 128
PAGES_PER_STEP = 32
ROW_CHUNKS = D_MODEL // HEAD_W
WORD_ROWS = ROW_CHUNKS // 2
VMEM_LIMIT = 56 * 1024 * 1024
SEQ_TILE = 512
TOKEN_TILE = 256
PEER_TILE = 128


def _tile(limit, n):
    return math.gcd(limit, n)

NT = (((1,), (1,)), ((), ()))
TN = (((0,), (0,)), ((), ()))


def _cparams(sem):
    return pltpu.CompilerParams(dimension_semantics=sem, vmem_limit_bytes=VMEM_LIMIT)


def _tree(op, xs):
    xs = list(xs)
    while len(xs) > 1:
        nxt = [op(xs[i], xs[i + 1]) for i in range(0, len(xs) - 1, 2)]
        if len(xs) % 2:
            nxt.append(xs[-1])
        xs = nxt
    return xs[0]


def _row_reduce(op, lane_reduce, x):
    blocks = [x[:, i * HEAD_W:(i + 1) * HEAD_W] for i in range(x.shape[1] // HEAD_W)]
    return lane_reduce(_tree(op, blocks), axis=-1, keepdims=True)


def _rms(x, g):
    return x * lax.rsqrt(jnp.mean(x * x, axis=-1, keepdims=True) + EPS) * g


def _store_head_rows(ref, x):
    n = x.shape[0]
    for h in range(HEADS):
        ref[pl.ds(h, n, stride=HEADS), :] = x[:, h * HEAD_W:(h + 1) * HEAD_W]


def _proj_kernel(x_ref, g_ref, w_ref, lb_ref, cos_ref, sin_ref,
                 hq_ref, hk_ref, lf_ref, hi_ref, gate_ref, k_ref, v_ref,
                 qb_ref, kb_ref, vb_ref, *, v_transposed):
    xb = _rms(x_ref[...], g_ref[...]).astype(BF16)

    def col(i):
        return jnp.dot(xb, w_ref[:, i * GROUP_W:(i + 1) * GROUP_W], preferred_element_type=F32)

    hq_ref[...] = col(0)
    lb = lb_ref[...]
    f = lb + (1.0 - lb) * jax.nn.sigmoid(col(1))
    lf_ref[...] = jnp.log(f)
    hk_ref[...] = 1.0 - f
    hi_ref[...] = col(2)
    gate_ref[...] = jax.nn.silu(col(3))

    cosf = cos_ref[...]
    sins = sin_ref[...]
    lane = lax.broadcasted_iota(I32, cosf.shape, 1)
    first_half = (lane % DA_DH) < (DA_DH // 2)

    def rope(x):
        swapped = jnp.where(first_half,
                            pltpu.roll(x, GROUP_W - DA_DH // 2, 1),
                            pltpu.roll(x, DA_DH // 2, 1))
        return x * cosf + swapped * sins

    q = rope(col(4))
    qb_ref[...] = (q * (DA_DH ** -0.5)).astype(BF16)
    k = rope(col(5))
    _store_head_rows(k_ref, k)
    kb_ref[...] = k.astype(BF16)
    v = col(6)
    _store_head_rows(v_ref, v)
    vb_ref[...] = (v.T if v_transposed else v).astype(BF16)


def _proj(x, g, w_bf, lb, cosf, sins, tm, v_transposed):
    T = x.shape[0]
    nl = cosf.shape[0] // tm
    row = lambda i: (i, 0)
    fixed = lambda i: (0, 0)
    tab = lambda i: (i % nl, 0)
    f32o = jax.ShapeDtypeStruct((T, GROUP_W), F32)
    bfo = jax.ShapeDtypeStruct((T, GROUP_W), BF16)
    ospec = pl.BlockSpec((tm, GROUP_W), row)
    vspec, vshape = ospec, bfo
    if v_transposed:
        vspec = pl.BlockSpec((GROUP_W, tm), lambda i: (0, i))
        vshape = jax.ShapeDtypeStruct((GROUP_W, T), BF16)
    return pl.pallas_call(
        functools.partial(_proj_kernel, v_transposed=v_transposed),
        grid=(T // tm,),
        in_specs=[pl.BlockSpec((tm, D_MODEL), row),
                  pl.BlockSpec((1, D_MODEL), fixed),
                  pl.BlockSpec(w_bf.shape, fixed),
                  pl.BlockSpec((1, GROUP_W), fixed),
                  pl.BlockSpec((tm, GROUP_W), tab),
                  pl.BlockSpec((tm, GROUP_W), tab)],
        out_specs=[ospec] * 5 + [pl.BlockSpec((tm * HEADS, HEAD_W), row)] * 2 + [ospec] * 2 + [vspec],
        out_shape=[f32o] * 5 + [jax.ShapeDtypeStruct((T * HEADS, HEAD_W), F32)] * 2 + [bfo] * 2 + [vshape],
        compiler_params=_cparams(("parallel",)),
        name="proj",
    )(x, g, w_bf, lb, cosf, sins)


def _hgrn_consts():
    C = HG_CHUNK
    t = np.arange(C)[:, None]
    u = np.arange(C)[None, :]
    mats = [(u <= t)]
    lows, pms = [], []
    h = C // 2
    while h >= 1:
        base = (t // (2 * h)) * (2 * h)
        r = base + h - 1
        lower = t >= base + h
        m = np.where(lower, (u > r) & (u <= t), (u > t) & (u <= r))
        mats.append(m)
        lows.append(np.broadcast_to(lower, (C, HEAD_W)))
        s = np.arange(C)[None, :]
        pms.append((t // (2 * h)) == (s // (2 * h)))
        h //= 2
    mats.append(u > t)
    wall = np.concatenate(mats, axis=0).astype(np.float32)
    return (jnp.asarray(wall, BF16), jnp.asarray(np.stack(lows), F32), jnp.asarray(np.stack(pms), F32))


def _split3(x):
    a = x.astype(BF16)
    r = x - a.astype(F32)
    b = r.astype(BF16)
    c = (r - b.astype(F32)).astype(BF16)
    return a, b, c


HGRN_HEADS_PER_STEP = 4


def _hgrn_kernel(q_ref, k_ref, v_ref, lf_ref, wall_ref, low_ref, pm_ref, o_ref, st_ref, st_scr, *, nlev):
    i = pl.program_id(2)
    C = HG_CHUNK
    nh = st_scr.shape[0]

    @pl.when(i == 0)
    def _():
        st_scr[...] = jnp.zeros_like(st_scr)

    wall = wall_ref[...]
    eye = (lax.broadcasted_iota(I32, (C, C), 0) == lax.broadcasted_iota(I32, (C, C), 1)).astype(F32)

    def intra(q, k, v, lf):
        lcat = jnp.concatenate(_split3(lf), axis=-1)
        d3 = jnp.dot(wall, lcat, preferred_element_type=F32)
        e_all = jnp.exp(d3[:, :HEAD_W] + d3[:, HEAD_W:2 * HEAD_W] + d3[:, 2 * HEAD_W:])
        e_g = e_all[0:C]
        e_last = e_all[C - 1:C]
        e_k = e_all[(nlev + 1) * C:(nlev + 2) * C]
        vb = v.astype(BF16)
        a = eye * jnp.sum(q * k, axis=-1, keepdims=True)
        for l in range(nlev):
            e_l = e_all[(l + 1) * C:(l + 2) * C]
            low = low_ref[l]
            ql = (q * e_l * low).astype(BF16)
            kl = (k * e_l * (1.0 - low)).astype(BF16)
            a = a + lax.dot_general(ql, kl, NT, preferred_element_type=F32) * pm_ref[l]
        o_intra = jnp.dot(a.astype(BF16), vb, preferred_element_type=F32)
        return (q * e_g).astype(BF16), o_intra, vb, (k * e_k).astype(BF16), e_last

    nchunks = q_ref.shape[0] // C
    parts = {}
    for c in range(nchunks):
        sl = slice(c * C, (c + 1) * C)
        for hh in range(nh):
            cols = slice(hh * HEAD_W, (hh + 1) * HEAD_W)
            parts[c, hh] = intra(q_ref[sl, cols], k_ref[sl, cols], v_ref[sl, cols], lf_ref[sl, cols])

    sts = [st_scr[hh] for hh in range(nh)]
    for c in range(nchunks):
        sl = slice(c * C, (c + 1) * C)
        for hh in range(nh):
            cols = slice(hh * HEAD_W, (hh + 1) * HEAD_W)
            qg, o_intra, vb, kd, e_last = parts[c, hh]
            o_ref[sl, cols] = o_intra + lax.dot_general(qg, sts[hh].astype(BF16), NT, preferred_element_type=F32)
            sts[hh] = e_last * sts[hh] + lax.dot_general(vb, kd, TN, preferred_element_type=F32)
    for hh in range(nh):
        st_scr[hh] = sts[hh]

    @pl.when(i == pl.num_programs(2) - 1)
    def _():
        for hh in range(nh):
            st_ref[0, hh] = sts[hh]


def _hgrn_prompt(hq, hk, hi, lf, B, L, lb_rows):
    wall, low, pm = _hgrn_consts()
    nlev = low.shape[0]
    nblk = L // lb_rows
    nh = HGRN_HEADS_PER_STEP
    blk = pl.BlockSpec((lb_rows, nh * HEAD_W), lambda b, h, i: (b * nblk + i, h))
    cst = lambda a: pl.BlockSpec(a.shape, lambda b, h, i: (0,) * a.ndim)
    return pl.pallas_call(
        functools.partial(_hgrn_kernel, nlev=nlev),
        grid=(B, HEADS // nh, nblk),
        in_specs=[blk, blk, blk, blk, cst(wall), cst(low), cst(pm)],
        out_specs=[blk, pl.BlockSpec((1, nh, HEAD_W, HEAD_W), lambda b, h, i: (b, h, 0, 0))],
        out_shape=[jax.ShapeDtypeStruct(hq.shape, F32),
                   jax.ShapeDtypeStruct((B, HEADS, HEAD_W, HEAD_W), F32)],
        scratch_shapes=[pltpu.VMEM((nh, HEAD_W, HEAD_W), F32)],
        compiler_params=_cparams(("parallel", "parallel", "arbitrary")),
        name="hgrn_prompt",
    )(hq, hk, hi, lf, wall, low, pm)


def _hgrn_step_kernel(qc_ref, kc_ref, lfc_ref, v_ref, s0_ref, o_ref, s_ref):
    for h in range(HEADS):
        qc, kc = qc_ref[0, h], kc_ref[0, h]
        dec = jnp.exp(lfc_ref[0, h])
        v = v_ref[0, h]
        s0 = s0_ref[0, h]
        s_ref[0, h] = dec * s0 + kc * v
        o_ref[0, h] = (jnp.sum((qc * dec) * s0, axis=0, keepdims=True)
                       + jnp.sum(qc * kc, axis=0, keepdims=True) * v)


def _hgrn_step(hq, hk, hi, lf, s0):
    B = hq.shape[0]
    colv = lambda a: a.reshape(B, HEADS, HEAD_W, 1)
    cspec = pl.BlockSpec((1, HEADS, HEAD_W, 1), lambda b: (b, 0, 0, 0))
    rspec = pl.BlockSpec((1, HEADS, 1, HEAD_W), lambda b: (b, 0, 0, 0))
    sspec = pl.BlockSpec((1, HEADS, HEAD_W, HEAD_W), lambda b: (b, 0, 0, 0))
    o, s = pl.pallas_call(
        _hgrn_step_kernel,
        grid=(B,),
        in_specs=[cspec, cspec, cspec, rspec, sspec],
        out_specs=[rspec, sspec],
        out_shape=[jax.ShapeDtypeStruct((B, HEADS, 1, HEAD_W), F32),
                   jax.ShapeDtypeStruct((B, HEADS, HEAD_W, HEAD_W), F32)],
        compiler_params=_cparams(("parallel",)),
        name="hgrn_step",
    )(colv(hq), colv(hk), colv(lf), hi.reshape(B, HEADS, 1, HEAD_W), s0)
    return o.reshape(B, GROUP_W), s


ATTN_QW = 128


def _attn_kernel(lam_ref, q_ref, k_ref, vt_ref, o_ref, qs_scr, m_scr, l_scr, acc_scr):
    qi = pl.program_id(1)
    ki = pl.program_id(2)
    bq = q_ref.shape[0]
    bk = k_ref.shape[0]
    nsub = 2 * bq // ATTN_QW

    @pl.when(ki == 0)
    def _():
        lane = lax.broadcasted_iota(I32, (bq, HEAD_W), 1)
        for h in range(HEADS):
            q = q_ref[:, h * HEAD_W:(h + 1) * HEAD_W]
            zero = jnp.zeros_like(q)
            qs_scr[h] = jnp.concatenate([jnp.where(lane < DA_DH, q, zero),
                                         jnp.where(lane >= DA_DH, q, zero)], axis=0)
        m_scr[...] = jnp.full(m_scr.shape, -jnp.inf, F32)
        l_scr[...] = jnp.zeros_like(l_scr)
        acc_scr[...] = jnp.zeros_like(acc_scr)

    @pl.when(ki <= qi)
    def _():
        kpos = lax.broadcasted_iota(I32, (bk, ATTN_QW), 0)
        qpos = lax.broadcasted_iota(I32, (bk, ATTN_QW), 1)
        off_diag = ki < qi
        for h in range(HEADS):
            kh = k_ref[:, h * HEAD_W:(h + 1) * HEAD_W]
            vth = vt_ref[h * HEAD_W:(h + 1) * HEAD_W, :]
            for sb in range(nsub):
                cols = slice(sb * ATTN_QW, (sb + 1) * ATTN_QW)
                st = lax.dot_general(kh, qs_scr[h, cols, :], NT, preferred_element_type=F32)
                keep = (kpos <= qpos + (sb * ATTN_QW) % bq) | off_diag
                st = jnp.where(keep, st, -jnp.inf)
                m_old = m_scr[h, :, cols]
                smax = _tree(jnp.maximum, [st[r * 8:(r + 1) * 8] for r in range(bk // 8)])
                m_new = jnp.maximum(m_old, jnp.max(smax, axis=0, keepdims=True))
                alpha = jnp.exp(m_old - m_new)
                pt = jnp.exp(st - m_new)
                psum = _tree(jnp.add, [pt[r * 8:(r + 1) * 8] for r in range(bk // 8)])
                l_scr[h, :, cols] = alpha * l_scr[h, :, cols] + jnp.sum(psum, axis=0, keepdims=True)
                acc_scr[h, :, cols] = (alpha * acc_scr[h, :, cols]
                                       + jnp.dot(vth, pt.astype(BF16), preferred_element_type=F32))
                m_scr[h, :, cols] = m_new

    @pl.when(ki == qi)
    def _():
        lam = lam_ref[0]
        for h in range(HEADS):
            r = acc_scr[h] / l_scr[h]
            o_ref[:, h * HEAD_W:(h + 1) * HEAD_W] = (r[:, :bq] - lam * r[:, bq:]).T


def _attn_prompt(lam, qb, kb, vbt, B, L, bq):
    nq = L // bq
    qspec = pl.BlockSpec((bq, GROUP_W), lambda b, qi, ki: (b * nq + qi, 0))
    kspec = pl.BlockSpec((bq, GROUP_W), lambda b, qi, ki: (b * nq + jnp.minimum(ki, qi), 0))
    vspec = pl.BlockSpec((GROUP_W, bq), lambda b, qi, ki: (0, b * nq + jnp.minimum(ki, qi)))
    return pl.pallas_call(
        _attn_kernel,
        grid=(B, nq, nq),
        in_specs=[pl.BlockSpec(memory_space=pltpu.SMEM), qspec, kspec, vspec],
        out_specs=qspec,
        out_shape=jax.ShapeDtypeStruct(qb.shape, F32),
        scratch_shapes=[pltpu.VMEM((HEADS, 2 * bq, HEAD_W), BF16),
                        pltpu.VMEM((HEADS, 1, 2 * bq), F32),
                        pltpu.VMEM((HEADS, 1, 2 * bq), F32),
                        pltpu.VMEM((HEADS, HEAD_W, 2 * bq), F32)],
        compiler_params=_cparams(("parallel", "parallel", "arbitrary")),
        name="attn_prompt",
    )(lam, qb, kb, vbt)


def _decode_kernel(pt_ref, lam_ref, q_ref, kn_ref, vn_ref, *rest):
    npg = PAGES_PER_STEP
    k_refs, v_refs = rest[:npg], rest[npg:2 * npg]
    o_ref, kcat, vcat, m_scr, l_scr, acc_scr = rest[2 * npg:]
    i = pl.program_id(1)
    nrow = 2 * HEADS

    rowi = lax.broadcasted_iota(I32, (nrow, GROUP_W), 0)
    lane = lax.broadcasted_iota(I32, (nrow, GROUP_W), 1)
    sel = (lane // HEAD_W == rowi // 2) & ((lane % HEAD_W) // DA_DH == rowi % 2)
    qbd = jnp.where(sel, jnp.broadcast_to(q_ref[0], (nrow, GROUP_W)), 0.0)

    @pl.when(i == 0)
    def _():
        m_scr[...] = jnp.full(m_scr.shape, -jnp.inf, F32)
        l_scr[...] = jnp.zeros_like(l_scr)
        acc_scr[...] = jnp.zeros_like(acc_scr)

    for p in range(npg):
        for h in range(HEADS):
            rows = slice(p * PAGE, (p + 1) * PAGE)
            cols = slice(h * HEAD_W, (h + 1) * HEAD_W)
            kcat[rows, cols] = k_refs[p][0, pl.ds(h, PAGE, stride=HEADS), :].astype(BF16)
            vcat[rows, cols] = v_refs[p][0, pl.ds(h, PAGE, stride=HEADS), :].astype(BF16)
    s = lax.dot_general(qbd.astype(BF16), kcat[...], NT, preferred_element_type=F32)
    m_old = m_scr[...]
    m_new = jnp.maximum(m_old, _row_reduce(jnp.maximum, jnp.max, s))
    alpha = jnp.exp(m_old - m_new)
    p_ = jnp.exp(s - m_new)
    l_scr[...] = alpha * l_scr[...] + _row_reduce(jnp.add, jnp.sum, p_)
    acc_scr[...] = alpha * acc_scr[...] + jnp.dot(p_.astype(BF16), vcat[...], preferred_element_type=F32)
    m_scr[...] = m_new

    @pl.when(i == pl.num_programs(1) - 1)
    def _():
        s_new = jnp.sum(qbd * kn_ref[0], axis=-1, keepdims=True)
        m_o = m_scr[...]
        m_n = jnp.maximum(m_o, s_new)
        al = jnp.exp(m_o - m_n)
        pn = jnp.exp(s_new - m_n)
        l_f = al * l_scr[...] + pn
        r = (al * acc_scr[...] + pn * vn_ref[0]) / l_f
        lam = lam_ref[0]
        outs = []
        for h in range(HEADS):
            blk = slice(h * HEAD_W, (h + 1) * HEAD_W)
            outs.append(r[2 * h:2 * h + 1, blk] - lam * r[2 * h + 1:2 * h + 2, blk])
        o_ref[0] = jnp.concatenate(outs, axis=-1)


def _attn_decode(page_table, lam, q, kn, vn, ck, cv):
    B = q.shape[0]
    nsteps = page_table.shape[1] // PAGES_PER_STEP
    tok = pl.BlockSpec((1, 1, GROUP_W), lambda b, i, pt: (b, 0, 0))

    def page(p):
        return pl.BlockSpec((1, PAGE * HEADS, HEAD_W),
                            lambda b, i, pt: (pt[b, i * PAGES_PER_STEP + p], 0, 0))

    pages = [page(p) for p in range(PAGES_PER_STEP)]
    grid_spec = pltpu.PrefetchScalarGridSpec(
        num_scalar_prefetch=1,
        grid=(B, nsteps),
        in_specs=[pl.BlockSpec(memory_space=pltpu.SMEM), tok, tok, tok] + pages + pages,
        out_specs=tok,
        scratch_shapes=[pltpu.VMEM((PAGES_PER_STEP * PAGE, GROUP_W), BF16),
                        pltpu.VMEM((PAGES_PER_STEP * PAGE, GROUP_W), BF16),
                        pltpu.VMEM((2 * HEADS, 1), F32),
                        pltpu.VMEM((2 * HEADS, 1), F32),
                        pltpu.VMEM((2 * HEADS, GROUP_W), F32)])
    r3 = lambda a: a.reshape(B, 1, GROUP_W)
    out = pl.pallas_call(
        _decode_kernel,
        grid_spec=grid_spec,
        out_shape=jax.ShapeDtypeStruct((B, 1, GROUP_W), F32),
        compiler_params=_cparams(("parallel", "arbitrary")),
        name="attn_decode",
    )(page_table, lam, r3(q), r3(kn), r3(vn), *([ck] * PAGES_PER_STEP), *([cv] * PAGES_PER_STEP))
    return out.reshape(B, GROUP_W)


def _mix_kernel(oh_ref, gate_ref, od_ref, x_ref, gh_ref, gd_ref, wo_ref, gf_ref, wq_ref,
                hp_ref, hn_ref, qh_ref, *, dscale):
    parts = []
    for h in range(HEADS):
        blk = slice(h * HEAD_W, (h + 1) * HEAD_W)
        parts.append(_rms(oh_ref[:, blk], gh_ref[...]) * gate_ref[:, blk])
    for h in range(HEADS):
        blk = slice(h * HEAD_W, (h + 1) * HEAD_W)
        parts.append(_rms(od_ref[:, blk], gd_ref[...]) * dscale)
    y = jnp.concatenate(parts, axis=-1).astype(BF16)
    hp = x_ref[...] + jnp.dot(y, wo_ref[...], preferred_element_type=F32)
    hp_ref[...] = hp
    hn = _rms(hp, gf_ref[...])
    hn_ref[...] = hn
    qh_ref[...] = jnp.dot(hn.astype(BF16), wq_ref[...], preferred_element_type=F32).astype(BF16)


def _mix(oh, gate, od, x, gh, gd, wo_bf, gf, wq_bf, dscale, tm):
    T = x.shape[0]
    row = lambda i: (i, 0)
    fixed = lambda i: (0, 0)
    half = pl.BlockSpec((tm, GROUP_W), row)
    full = pl.BlockSpec((tm, D_MODEL), row)
    cst = lambda a: pl.BlockSpec(a.shape, fixed)
    nq = wq_bf.shape[1]
    return pl.pallas_call(
        functools.partial(_mix_kernel, dscale=dscale),
        grid=(T // tm,),
        in_specs=[half, half, half, full, cst(gh), cst(gd), cst(wo_bf), cst(gf), cst(wq_bf)],
        out_specs=[full, full, pl.BlockSpec((tm, nq), row)],
        out_shape=[jax.ShapeDtypeStruct((T, D_MODEL), F32), jax.ShapeDtypeStruct((T, D_MODEL), F32),
                   jax.ShapeDtypeStruct((T, nq), BF16)],
        compiler_params=_cparams(("parallel",)),
        name="mix",
    )(oh, gate, od, x, gh, gd, wo_bf, gf, wq_bf)


def _staircase():
    K = PEER_TOPK
    return [(a, b) for a in range(K) for b in range(K) if (a + 1) * (b + 1) <= K]


def _topk_kernel(qh_ref, sk_ref, e_ref, g_ref, s_scr, val_scr, idx_scr, cand_scr, cidx_scr, sc_scr, e_scr):
    K = PEER_TOPK
    NK = PEER_NKEYS
    tb = qh_ref.shape[0]
    half_w = PEER_HEADS * NK
    neg = -jnp.inf

    for c in range(2):
        qc = qh_ref[:, c * half_w:(c + 1) * half_w]
        s = lax.dot_general(sk_ref[c], qc, NT, preferred_element_type=F32)
        s_scr[...] = s.reshape(NK, PEER_HEADS, tb)

        def body(a, carry):
            sv = [s_scr[n] for n in range(NK)]
            m = _tree(jnp.maximum, sv)
            idx = _tree(jnp.minimum, [jnp.where(sv[n] == m, float(n), float(NK)) for n in range(NK)])
            for n in range(NK):
                s_scr[n] = jnp.where(idx == float(n), neg, sv[n])
            val_scr[c, a] = m
            idx_scr[c, a] = idx.astype(I32)
            return carry

        lax.fori_loop(0, K, body, 0)

    pairs = _staircase()
    for i, (a, b) in enumerate(pairs):
        cand_scr[i] = val_scr[0, a] + val_scr[1, b]
        cidx_scr[i] = (idx_scr[0, a] * NK + idx_scr[1, b]) * WORD_ROWS
    flats = [float(a * K + b) for a, b in pairs]
    big = float(K * K)

    def body2(r, carry):
        cs = [cand_scr[i] for i in range(len(pairs))]
        m = _tree(jnp.maximum, cs)
        pos = _tree(jnp.minimum, [jnp.where(cv == m, fl, big) for cv, fl in zip(cs, flats)])
        picks = []
        for i, (cv, fl) in enumerate(zip(cs, flats)):
            hit = pos == fl
            cand_scr[i] = jnp.where(hit, neg, cv)
            picks.append(jnp.where(hit, cidx_scr[i], 0))
        sc_scr[r] = m
        e_scr[r] = _tree(jnp.maximum, picks)
        return carry

    lax.fori_loop(0, K, body2, 0)
    sc = sc_scr[...]
    ex = jnp.exp(sc - sc[0:1])
    g = ex / jnp.sum(ex, axis=0, keepdims=True)
    g_ref[...] = g.reshape(PEER_SEL, tb).T
    e_ref[...] = e_scr[...].reshape(PEER_SEL, tb).T


def _topk(qh, sk_big, tb):
    T = qh.shape[0]
    ncand = len(_staircase())
    hw = (PEER_HEADS, tb)
    return pl.pallas_call(
        _topk_kernel,
        grid=(T // tb,),
        in_specs=[pl.BlockSpec((tb, qh.shape[1]), lambda i: (i, 0)),
                  pl.BlockSpec(sk_big.shape, lambda i: (0, 0, 0))],
        out_specs=[pl.BlockSpec((tb, PEER_SEL), lambda i: (i, 0))] * 2,
        out_shape=[jax.ShapeDtypeStruct((T, PEER_SEL), I32),
                   jax.ShapeDtypeStruct((T, PEER_SEL), F32)],
        scratch_shapes=[pltpu.VMEM((PEER_NKEYS,) + hw, F32),
                        pltpu.VMEM((2, PEER_TOPK) + hw, F32),
                        pltpu.VMEM((2, PEER_TOPK) + hw, I32),
                        pltpu.VMEM((ncand,) + hw, F32),
                        pltpu.VMEM((ncand,) + hw, I32),
                        pltpu.VMEM((PEER_TOPK,) + hw, F32),
                        pltpu.VMEM((PEER_TOPK,) + hw, I32)],
        compiler_params=_cparams(("parallel",)),
        name="peer_topk",
    )(qh, sk_big)


def _pack_kernel(t_ref, o_ref):
    n = t_ref.shape[0]

    def bits(x):
        return pltpu.bitcast(x.astype(BF16).astype(F32), jnp.uint32) >> 16

    for s in range(WORD_ROWS):
        lo = bits(t_ref[:, (2 * s) * HEAD_W:(2 * s + 1) * HEAD_W])
        hi = bits(t_ref[:, (2 * s + 1) * HEAD_W:(2 * s + 2) * HEAD_W])
        o_ref[pl.ds(s, n, stride=WORD_ROWS), :] = pltpu.bitcast((hi << 16) | lo, I32)


def _pack_table(tab, rows=256):
    n = tab.shape[0]
    return pl.pallas_call(
        _pack_kernel,
        grid=(n // rows,),
        in_specs=[pl.BlockSpec((rows, D_MODEL), lambda i: (i, 0))],
        out_specs=pl.BlockSpec((rows * WORD_ROWS, HEAD_W), lambda i: (i, 0)),
        out_shape=jax.ShapeDtypeStruct((n * WORD_ROWS, HEAD_W), I32),
        compiler_params=_cparams(("parallel",)),
        name="pack_table",
    )(tab)


IDX_GROUP = 32
IDX_SLOTS = 2
STG_BUFS = 2


def _stream_index_groups(idx_hbm, idx_smem, sem, tb, n_groups, group_fn):
    step = pl.program_id(0)
    grp = idx_smem.shape[1]
    gps = tb // grp
    slots = min(IDX_SLOTS, gps)
    assert gps % slots == 0

    def copy(group, slot):
        return pltpu.make_async_copy(idx_hbm.at[pl.ds(group * grp, grp)],
                                     idx_smem.at[slot], sem.at[slot])

    @pl.when(step == 0)
    def _():
        for s in range(slots):
            copy(s, s).start()

    def body(it, carry):
        for s in range(slots):
            gl = it * slots + s
            g = step * gps + gl
            copy(g, s).wait()
            group_fn(gl, idx_smem.at[s])

            @pl.when(g + slots < n_groups)
            def _():
                copy(g + slots, s).start()
        return carry

    lax.fori_loop(0, gps // slots, body, 0)


def _gather_rows(idx8, i, tab_ref, stg):
    for j in range(PEER_SEL):
        r = pl.multiple_of(idx8[i, j], WORD_ROWS)
        stg[j * WORD_ROWS:(j + 1) * WORD_ROWS, :] = tab_ref[pl.ds(r, WORD_ROWS), :]


def _peer_u_kernel(idx_hbm, x_ref, g_ref, tab_ref, dmask_ref, gsum_ref, expand_ref, w_ref,
                   stg, c_scr, idx_smem, sem, *, n_groups):
    tb = x_ref.shape[0]
    grp = idx_smem.shape[1]
    dmask = dmask_ref[...]

    def group(gl, idx8):
        for sub in range(grp // 8):
            rows = []
            for i in range(sub * 8, sub * 8 + 8):
                buf = stg.at[i % STG_BUFS]
                _gather_rows(idx8, i, tab_ref, buf)
                ub = pltpu.bitcast(buf[...], BF16)
                xt = x_ref[gl * grp +i].astype(BF16)
                r = lax.dot_general(xt, ub, NT, preferred_element_type=F32)
                rows.append(jnp.sum(r * dmask, axis=0, keepdims=True))
            row0 = pl.multiple_of(gl * grp +sub * 8, 8)
            c_scr[pl.ds(row0, 8), :] = jnp.concatenate(rows, axis=0)

    _stream_index_groups(idx_hbm, idx_smem, sem, tb, n_groups, group)
    c = c_scr[...]
    c_hi = c.astype(BF16)
    c_lo = (c - c_hi.astype(F32)).astype(BF16)
    a = (jnp.dot(c_hi, gsum_ref[...], preferred_element_type=F32)
         + jnp.dot(c_lo, gsum_ref[...], preferred_element_type=F32))
    w = g_ref[...] * jax.nn.gelu(a)
    w_ref[...] = jnp.dot(w.astype(BF16), expand_ref[...], preferred_element_type=F32)


def _peer_v_kernel(idx_hbm, w_ref, tab_ref, dmask_ref, o_ref, stg, idx_smem, sem, *, n_groups):
    tb = w_ref.shape[0]
    grp = idx_smem.shape[1]
    dmask = dmask_ref[...]

    def group(gl, idx8):
        for sub in range(grp // 8):
            w8 = w_ref[pl.ds(pl.multiple_of(gl * grp +sub * 8, 8), 8), :]
            for k in range(8):
                i = sub * 8 + k
                buf = stg.at[i % STG_BUFS]
                _gather_rows(idx8, i, tab_ref, buf)
                vb = pltpu.bitcast(buf[...], BF16)
                lhs = (jnp.broadcast_to(w8[k:k + 1, :], dmask.shape) * dmask).astype(BF16)
                kq = lhs.shape[1] // 4
                o_ref[gl * grp +i] = _tree(jnp.add, [
                    jnp.dot(lhs[:, q * kq:(q + 1) * kq], vb[q * kq:(q + 1) * kq], preferred_element_type=F32)
                    for q in range(4)])

    _stream_index_groups(idx_hbm, idx_smem, sem, tb, n_groups, group)


def _peer_consts():
    sel_w = PEER_SEL * ROW_CHUNKS
    lane = np.arange(sel_w)
    dmask = (lane[None, :] % ROW_CHUNKS == np.arange(ROW_CHUNKS)[:, None]).astype(np.float32)
    gsum = (lane[:, None] // ROW_CHUNKS == np.arange(PEER_SEL)[None, :]).astype(np.float32)
    return jnp.asarray(dmask, F32), jnp.asarray(gsum, BF16), jnp.asarray(gsum.T, BF16)


def _peer_u(idx, hn, g, tab_u, tb):
    T = hn.shape[0]
    dmask, gsum, expand = _peer_consts()
    sel_w = PEER_SEL * ROW_CHUNKS
    cst = lambda a: pl.BlockSpec(a.shape, lambda i: (0, 0))
    return pl.pallas_call(
        functools.partial(_peer_u_kernel, n_groups=T // min(IDX_GROUP, tb)),
        grid=(T // tb,),
        in_specs=[pl.BlockSpec(memory_space=pl.ANY),
                  pl.BlockSpec((tb, ROW_CHUNKS, HEAD_W), lambda i: (i, 0, 0)),
                  pl.BlockSpec((tb, PEER_SEL), lambda i: (i, 0)),
                  pl.BlockSpec(memory_space=pltpu.VMEM),
                  cst(dmask), cst(gsum), cst(expand)],
        out_specs=pl.BlockSpec((tb, sel_w), lambda i: (i, 0)),
        out_shape=jax.ShapeDtypeStruct((T, sel_w), F32),
        scratch_shapes=[pltpu.VMEM((STG_BUFS, PEER_SEL * WORD_ROWS, HEAD_W), I32),
                        pltpu.VMEM((tb, sel_w), F32),
                        pltpu.SMEM((IDX_SLOTS, min(IDX_GROUP, tb), PEER_SEL), I32),
                        pltpu.SemaphoreType.DMA((IDX_SLOTS,))],
        compiler_params=_cparams(("arbitrary",)),
        name="peer_u",
    )(idx, hn.reshape(T, ROW_CHUNKS, HEAD_W), g, tab_u, dmask, gsum, expand)


def _peer_v(idx, wexp, tab_v, tb):
    T = wexp.shape[0]
    dmask, _, _ = _peer_consts()
    out = pl.pallas_call(
        functools.partial(_peer_v_kernel, n_groups=T // min(IDX_GROUP, tb)),
        grid=(T // tb,),
        in_specs=[pl.BlockSpec(memory_space=pl.ANY),
                  pl.BlockSpec((tb, wexp.shape[1]), lambda i: (i, 0)),
                  pl.BlockSpec(memory_space=pltpu.VMEM),
                  pl.BlockSpec(dmask.shape, lambda i: (0, 0))],
        out_specs=pl.BlockSpec((tb, ROW_CHUNKS, HEAD_W), lambda i: (i, 0, 0)),
        out_shape=jax.ShapeDtypeStruct((T, ROW_CHUNKS, HEAD_W), F32),
        scratch_shapes=[pltpu.VMEM((STG_BUFS, PEER_SEL * WORD_ROWS, HEAD_W), I32),
                        pltpu.SMEM((IDX_SLOTS, min(IDX_GROUP, tb), PEER_SEL), I32),
                        pltpu.SemaphoreType.DMA((IDX_SLOTS,))],
        compiler_params=_cparams(("arbitrary",)),
        name="peer_v",
    )(idx, wexp, tab_v, dmask)
    return out.reshape(T, D_MODEL)


def _ple_kernel(hp_ref, pe_ref, p_ref, wg_ref, bg_ref, wp_ref, gf_ref, y_ref):
    h = hp_ref[...] + pe_ref[...]
    gate = jax.nn.sigmoid(jnp.dot(h.astype(BF16), wg_ref[...], preferred_element_type=F32) + bg_ref[...])
    h = h + gate * jnp.dot(p_ref[...].astype(BF16), wp_ref[...], preferred_element_type=F32)
    y_ref[...] = _rms(h, gf_ref[...])


def _ple(hp, pe, p, wg_bf, bg, wp_bf, gf, tm):
    T = hp.shape[0]
    row = lambda i: (i, 0)
    full = pl.BlockSpec((tm, D_MODEL), row)
    cst = lambda a: pl.BlockSpec(a.shape, lambda i: (0, 0))
    return pl.pallas_call(
        _ple_kernel,
        grid=(T // tm,),
        in_specs=[full, full, pl.BlockSpec((tm, PLE_DIM), row), cst(wg_bf), cst(bg), cst(wp_bf), cst(gf)],
        out_specs=full,
        out_shape=jax.ShapeDtypeStruct((T, D_MODEL), F32),
        compiler_params=_cparams(("parallel",)),
        name="ple",
    )(hp, pe, p, wg_bf, bg, wp_bf, gf)


def _rope_tables(pos):
    half = DA_DH // 2
    freqs = ROPE_THETA ** (-jnp.arange(half, dtype=F32) / half)
    ang = pos.astype(F32)[:, None] * freqs[None, :]
    cos = jnp.tile(jnp.cos(ang), (1, GROUP_W // half))
    sign = jnp.where((jnp.arange(GROUP_W) % DA_DH) < half, -1.0, 1.0).astype(F32)
    sin = jnp.tile(jnp.sin(ang), (1, GROUP_W // half)) * sign[None, :]
    return cos, sin


def _ffn(hp_parts, p, weights):
    hp, hn, qh = hp_parts
    (sk_big, tab_u, tab_v, wg_bf, bg, wp_bf, gfinal) = weights
    T = hp.shape[0]
    tk = TOKEN_TILE
    tb_peer = _tile(PEER_TILE, T)
    tpad = -(-T // tk) * tk
    qh_p = jnp.pad(qh, ((0, tpad - T), (0, 0))) if tpad != T else qh
    idx, gw = _topk(qh_p, sk_big, tk)
    idx, gw = idx[:T], gw[:T]
    wexp = _peer_u(idx, hn, gw, tab_u, tb_peer)
    pe = _peer_v(idx, wexp, tab_v, tb_peer)
    tm = _tile(TOKEN_TILE, T)
    return _ple(hp, pe, p, wg_bf, bg, wp_bf, gfinal, tm)


def kernel(x_prompt, x_sample, p_prompt, p_sample, cache_k, cache_v, state_hgrn, page_table, g_attn, w_in, hgrn_gamma, g_hgrn_norm, lambda_q1, lambda_k1, lambda_q2, lambda_k2, g_diff_norm, w_out, g_ffn, peer_w_query, peer_sub_keys, peer_u, peer_v, ple_w_gate, ple_b_gate, ple_w_proj, g_final):
    Bp, Lp, D = x_prompt.shape
    Bs, Ls, _ = x_sample.shape
    assert D == D_MODEL and Ls == 1 and w_in.shape[0] == 1
    l = 0
    past_len = page_table.shape[1] * cache_k.shape[2]
    lam_init = 0.8 - 0.6 * math.exp(-0.3 * l)
    lam = (jnp.exp(jnp.sum(lambda_q1[l] * lambda_k1[l])) - jnp.exp(jnp.sum(lambda_q2[l] * lambda_k2[l]))
           + lam_init).reshape(1).astype(F32)
    lb = jnp.cumsum(jax.nn.softmax(hgrn_gamma.astype(F32), axis=0), axis=0)[l].reshape(1, GROUP_W)

    w_in_bf = w_in[l].astype(BF16)
    w_out_bf = w_out[l].astype(BF16)
    nqc = PEER_HEADS * 2 * PEER_NKEYS
    wq_bf = (peer_w_query[l].reshape(D, PEER_HEADS, 2, PEER_NKEYS).transpose(0, 2, 1, 3)
             .reshape(D, nqc).astype(BF16))
    sk_big = jnp.einsum('hcnk,hg->cnhgk', peer_sub_keys[l], jnp.eye(PEER_HEADS, dtype=F32)).reshape(
        2, PEER_NKEYS * PEER_HEADS, PEER_HEADS * PEER_NKEYS).astype(BF16)
    tab_u = _pack_table(peer_u[l])
    tab_v = _pack_table(peer_v[l])
    wg_bf = ple_w_gate[l].astype(BF16)
    wp_bf = ple_w_proj[l].astype(BF16)
    row = lambda a: a.reshape(1, -1).astype(F32)
    ffn_w = (sk_big, tab_u, tab_v, wg_bf, row(ple_b_gate[l]), wp_bf, row(g_final))

    def group(x, pos_tab, tm, v_transposed):
        cosf, sins = pos_tab
        return _proj(x, row(g_attn[l]), w_in_bf, lb, cosf, sins, tm, v_transposed)

    def mix(oh, gate, od, x, tm):
        return _mix(oh, gate, od, x, row(g_hgrn_norm[l]), row(g_diff_norm[l]), w_out_bf, row(g_ffn[l]),
                    wq_bf, 1.0 - lam_init, tm)

    Tp = Bp * Lp
    xp = x_prompt.reshape(Tp, D)
    seq_tile = _tile(SEQ_TILE, Lp)
    hq, hk, lf, hi, gate, k_p, v_p, qb, kb, vb = group(xp, _rope_tables(jnp.arange(Lp)), seq_tile, True)
    o_h, st_p = _hgrn_prompt(hq, hk, hi, lf, Bp, Lp, seq_tile)
    o_d = _attn_prompt(lam, qb, kb, vb, Bp, Lp, seq_tile)
    y_p = _ffn(mix(o_h, gate, o_d, xp, _tile(TOKEN_TILE, Tp)), p_prompt.reshape(Tp, PLE_DIM), ffn_w)

    xs = x_sample.reshape(Bs, D)
    pos_s = jnp.full((Bs,), past_len, dtype=jnp.int32)
    hq, hk, lf, hi, gate, k_s, v_s, qb, kb, vb = group(xs, _rope_tables(pos_s), Bs, False)
    o_h, st_s = _hgrn_step(hq, hk, hi, lf, state_hgrn.reshape(Bs, HEADS, HEAD_W, HEAD_W))
    npool = cache_k.shape[1]
    o_d = _attn_decode(page_table, lam, qb.astype(F32), k_s.reshape(Bs, GROUP_W), v_s.reshape(Bs, GROUP_W),
                       cache_k.reshape(npool, PAGE * HEADS, HEAD_W),
                       cache_v.reshape(npool, PAGE * HEADS, HEAD_W))
    y_s = _ffn(mix(o_h, gate, o_d, xs, _tile(TOKEN_TILE, Bs)), p_sample.reshape(Bs, PLE_DIM), ffn_w)

    hd = (HEADS, HEAD_W)
    return (y_p.reshape(Bp, Lp, D), y_s.reshape(Bs, Ls, D),
            k_p.reshape((1, Bp, Lp) + hd), v_p.reshape((1, Bp, Lp) + hd),
            jnp.swapaxes(st_p, -1, -2)[None],
            k_s.reshape((1, Bs, Ls) + hd), v_s.reshape((1, Bs, Ls) + hd), st_s[None])
```
